```python
import jax, jax.numpy as jnp
from jax import lax
import numpy as np

D_MODEL = 1024
BATCH = 8
SEQ = 2048
DEPTH = 4
DEC_BATCH = 32
DEC_SEQ = 1
PAST_LEN = 16384
PAGE_SIZE = 128

PLE_DIM = 256
EPS = 1e-6
ROPE_BASE = 10000.0
CONV_DIM = 512
CONV_WIDTH = 31
CONV_STATE = CONV_WIDTH - 1
RET_HEADS = 4
RET_DK = 128
RET_DV = 256
RET_CHUNK = 128
MLA_HEADS = 8
MLA_Q_LORA = 384
MLA_KV_LORA = 256
MLA_NOPE = 64
MLA_ROPE = 32
MLA_V = 64
MLA_QK = MLA_NOPE + MLA_ROPE
MLA_QBLOCK = 128
D_FF = -(-8 * D_MODEL // (3 * 256)) * 256
IN_SIZES = (2 * CONV_DIM, RET_HEADS * RET_DK, RET_HEADS * RET_DK, RET_HEADS * RET_DV, RET_HEADS * RET_DV,
            MLA_Q_LORA, MLA_KV_LORA, MLA_ROPE, 3 * D_MODEL)
IN_DIM = sum(IN_SIZES)

kernel_name = 'hybrid_conv_retention_mla_decoder_step'


def rms_norm(x, g):
    xf = x.astype(jnp.float32)
    y = xf * lax.rsqrt(jnp.mean(xf * xf, axis=-1, keepdims=True) + EPS)
    return (y * g.astype(jnp.float32)).astype(x.dtype)


def layer_norm(x, g, b):
    xf = x.astype(jnp.float32)
    mu = jnp.mean(xf, axis=-1, keepdims=True)
    var = jnp.mean(jnp.square(xf - mu), axis=-1, keepdims=True)
    y = (xf - mu) * lax.rsqrt(var + EPS)
    return (y * g.astype(jnp.float32) + b.astype(jnp.float32)).astype(x.dtype)


def rope(x, pos):
    half = x.shape[-1] // 2
    inv = ROPE_BASE ** (-jnp.arange(half, dtype=jnp.float32) / half)
    ang = pos.astype(jnp.float32)[:, None] * inv[None, :]
    c = jnp.cos(ang)[:, None, :]
    s = jnp.sin(ang)[:, None, :]
    x1 = x[..., :half].astype(jnp.float32)
    x2 = x[..., half:].astype(jnp.float32)
    return jnp.concatenate([x1 * c - x2 * s, x1 * s + x2 * c], axis=-1).astype(x.dtype)


def causal_dwconv(u_ext, w, b):
    y = lax.conv_general_dilated(u_ext, w[:, None, :].astype(u_ext.dtype), window_strides=(1,), padding='VALID',
                                 dimension_numbers=('NWC', 'WIO', 'NWC'), feature_group_count=u_ext.shape[-1])
    return y + b.astype(y.dtype)


def conv_branch(glu_in, prefix, w_dw, b_dw, g_ln, b_ln, w_proj):
    a, gate = jnp.split(glu_in, 2, axis=-1)
    u = a * jax.nn.sigmoid(gate)
    u_ext = jnp.concatenate([prefix.astype(u.dtype), u], axis=1)
    y = jax.nn.silu(layer_norm(causal_dwconv(u_ext, w_dw, b_dw), g_ln, b_ln))
    return y @ w_proj, u_ext[:, -CONV_STATE:]


def retention_log_decay():
    return jnp.log(1.0 - 2.0 ** (-5.0 - jnp.arange(RET_HEADS, dtype=jnp.float32)))


def retention_chunk(S, q, k, v, log_gamma):
    C = q.shape[1]
    idx = jnp.arange(C, dtype=jnp.float32)
    diff = idx[:, None] - idx[None, :]
    decay = jnp.where(diff >= 0, jnp.exp(jnp.maximum(diff, 0.0)[None] * log_gamma[:, None, None]), 0.0)
    scores = jnp.einsum('bihd,bjhd->bhij', q, k) * decay[None]
    o = jnp.einsum('bhij,bjhe->bihe', scores, v)
    cross = jnp.exp((idx + 1.0)[:, None] * log_gamma[None, :])
    o = o + jnp.einsum('bihd,bhde->bihe', q, S) * cross[None, :, :, None]
    kdec = jnp.exp((C - 1.0 - idx)[:, None] * log_gamma[None, :])
    S_new = jnp.exp(C * log_gamma)[None, :, None, None] * S + jnp.einsum('bjhd,bjhe->bhde', k * kdec[None, :, :, None], v)
    return S_new, o


def retention_prompt(q, k, v, log_gamma):
    B, T, H, dk = q.shape
    dv = v.shape[-1]
    n = T // RET_CHUNK
    to_chunks = lambda a: jnp.moveaxis(a.reshape(B, n, RET_CHUNK, H, a.shape[-1]), 1, 0)
    S0 = jnp.zeros((B, H, dk, dv), jnp.float32)
    S, o = lax.scan(lambda s, c: retention_chunk(s, c[0], c[1], c[2], log_gamma), S0,
                    (to_chunks(q), to_chunks(k), to_chunks(v)))
    return S, jnp.moveaxis(o, 0, 1).reshape(B, T, H, dv)


def head_group_norm(o, g):
    of = o.astype(jnp.float32)
    mu = jnp.mean(of, axis=-1, keepdims=True)
    var = jnp.mean(jnp.square(of - mu), axis=-1, keepdims=True)
    y = ((of - mu) * lax.rsqrt(var + EPS)).reshape(o.shape[0], o.shape[1], -1)
    return y * g.astype(jnp.float32)


def mla_keys_values(c_kv, k_pe, w_uk, w_uv, g_kn):
    B, L, _ = c_kv.shape
    k_nope = (c_kv @ w_uk).reshape(B, L, MLA_HEADS, MLA_NOPE)
    k = jnp.concatenate([k_nope, jnp.broadcast_to(k_pe[:, :, None, :], (B, L, MLA_HEADS, MLA_ROPE)).astype(k_nope.dtype)], axis=-1)
    k = rms_norm(k, g_kn)
    v = (c_kv @ w_uv).reshape(B, L, MLA_HEADS, MLA_V)
    return k, v


def attend(q, k, v, q_pos, k_pos):
    s = jnp.einsum('bqhd,bkhd->bhqk', q, k, preferred_element_type=jnp.float32) * (MLA_QK ** -0.5)
    s = jnp.where(k_pos[None, None, None, :] <= q_pos[None, None, :, None], s, -1e30)
    p = jax.nn.softmax(s, axis=-1).astype(v.dtype)
    return jnp.einsum('bhqk,bkhd->bqhd', p, v)


def attend_blocked(q, k, v, pos):
    B, T, H, d = q.shape
    nb = T // MLA_QBLOCK
    qb = jnp.moveaxis(q.reshape(B, nb, MLA_QBLOCK, H, d), 1, 0)
    pb = pos.reshape(nb, MLA_QBLOCK)
    ob = lax.map(lambda a: attend(a[0], k, v, a[1], pos), (qb, pb))
    return jnp.moveaxis(ob, 0, 1).reshape(B, T, H, v.shape[-1])


def decoder_layer(x, p_l, pos, conv_prefix, ret_state, past_ckv, past_kpe, W, l, is_prompt):
    B, T, _ = x.shape
    h = rms_norm(x, W['g_mix'][l])
    z = h @ W['w_in'][l]
    split_idx = np.cumsum(IN_SIZES)[:-1].tolist()
    glu_in, rq, rk, rv, rg, q_a, kv_a, k_r, gates = jnp.split(z, split_idx, axis=-1)

    y_a, conv_new = conv_branch(glu_in, conv_prefix, W['w_dw'][l], W['b_dw'][l], W['g_cln'][l], W['b_cln'][l], W['w_conv_out'][l])

    log_gamma = retention_log_decay()
    q = rope(rq.reshape(B, T, RET_HEADS, RET_DK), pos)
    k = rope(rk.reshape(B, T, RET_HEADS, RET_DK), pos) * (RET_DK ** -0.5)
    v = rv.reshape(B, T, RET_HEADS, RET_DV)
    if is_prompt:
        S_new, o = retention_prompt(q, k, v, log_gamma)
    else:
        S_new, o = retention_chunk(ret_state.astype(jnp.float32), q, k, v, log_gamma)
    o = head_group_norm(o, W['g_rgn'][l]).astype(x.dtype)
    y_b = (o * jax.nn.silu(rg)) @ W['w_ret_out'][l]

    c_q = rms_norm(q_a, W['g_qa'][l])
    qm = (c_q @ W['w_uq'][l]).reshape(B, T, MLA_HEADS, MLA_QK)
    qm = jnp.concatenate([qm[..., :MLA_NOPE], rope(qm[..., MLA_NOPE:], pos)], axis=-1)
    qm = rms_norm(qm, W['g_qn'][l])
    c_kv = rms_norm(kv_a, W['g_kva'][l])
    k_pe = rope(k_r[:, :, None, :], pos)[:, :, 0, :]
    if is_prompt:
        km, vm = mla_keys_values(c_kv, k_pe, W['w_uk'][l], W['w_uv'][l], W['g_kn'][l])
        om = attend_blocked(qm, km, vm, pos)
    else:
        past_len = past_ckv.shape[1]
        all_ckv = jnp.concatenate([past_ckv.astype(c_kv.dtype), c_kv], axis=1)
        all_kpe = jnp.concatenate([past_kpe.astype(k_pe.dtype), k_pe], axis=1)
        k_pos = jnp.concatenate([jnp.arange(past_len, dtype=jnp.int32), pos])
        km, vm = mla_keys_values(all_ckv, all_kpe, W['w_uk'][l], W['w_uv'][l], W['g_kn'][l])
        om = attend(qm, km, vm, pos, k_pos)
    y_c = om.reshape(B, T, MLA_HEADS * MLA_V) @ W['w_mla_out'][l]

    g_a, g_b, g_c = jnp.split(jax.nn.sigmoid(gates), 3, axis=-1)
    x = x + (g_a * y_a + g_b * y_b + g_c * y_c) @ W['w_out'][l]

    hf = rms_norm(x, W['g_ffn'][l])
    x = x + (jax.nn.silu(hf @ W['w_gate'][l]) * (hf @ W['w_up'][l])) @ W['w_down'][l]

    pg = jax.nn.sigmoid(rms_norm(x, W['g_ple'][l]) @ W['w_ple_gate'][l])
    x = x + pg * (p_l @ W['w_ple_proj'][l])
    return x, c_kv, k_pe, S_new, conv_new


def setup_inputs(seed: int = 0) -> dict:
    key = jax.random.key(seed)
    ks = iter(jax.random.split(key, 48))
    f32 = jnp.float32

    def nrm(shape, scale):
        return scale * jax.random.normal(next(ks), shape, f32)

    def gain(n):
        return 1.0 + nrm((DEPTH, n), 0.02)

    n_pages = PAST_LEN // PAGE_SIZE
    n_used = DEC_BATCH * n_pages
    n_phys = n_used + max(1, n_used // 4)
    d = D_MODEL
    inp = {}
    inp['x_prompt'] = nrm((BATCH, SEQ, d), 1.0)
    inp['x_sample'] = nrm((DEC_BATCH, DEC_SEQ, d), 1.0)
    inp['p_prompt'] = nrm((DEPTH, BATCH, SEQ, PLE_DIM), 1.0)
    inp['p_sample'] = nrm((DEPTH, DEC_BATCH, DEC_SEQ, PLE_DIM), 1.0)
    inp['cache_ckv'] = nrm((DEPTH, n_phys, PAGE_SIZE, MLA_KV_LORA), 1.0)
    inp['cache_kpe'] = nrm((DEPTH, n_phys, PAGE_SIZE, MLA_ROPE), 1.0)
    inp['state_ret'] = nrm((DEPTH, DEC_BATCH, RET_HEADS, RET_DK, RET_DV), 0.5)
    inp['state_conv'] = nrm((DEPTH, DEC_BATCH, CONV_STATE, CONV_DIM), 0.5)
    inp['page_table'] = jax.random.permutation(next(ks), n_phys)[:n_used].reshape(DEC_BATCH, n_pages).astype(jnp.int32)
    inp['g_mix'] = gain(d)
    inp['w_in'] = nrm((DEPTH, d, IN_DIM), d ** -0.5)
    inp['w_dw'] = nrm((DEPTH, CONV_WIDTH, CONV_DIM), CONV_WIDTH ** -0.5)
    inp['b_dw'] = nrm((DEPTH, CONV_DIM), 0.02)
    inp['g_cln'] = gain(CONV_DIM)
    inp['b_cln'] = nrm((DEPTH, CONV_DIM), 0.02)
    inp['w_conv_out'] = nrm((DEPTH, CONV_DIM, d), CONV_DIM ** -0.5)
    inp['g_rgn'] = gain(RET_HEADS * RET_DV)
    inp['w_ret_out'] = nrm((DEPTH, RET_HEADS * RET_DV, d), (RET_HEADS * RET_DV) ** -0.5)
    inp['g_qa'] = gain(MLA_Q_LORA)
    inp['w_uq'] = nrm((DEPTH, MLA_Q_LORA, MLA_HEADS * MLA_QK), MLA_Q_LORA ** -0.5)
    inp['g_qn'] = gain(MLA_QK)
    inp['g_kva'] = gain(MLA_KV_LORA)
    inp['w_uk'] = nrm((DEPTH, MLA_KV_LORA, MLA_HEADS * MLA_NOPE), MLA_KV_LORA ** -0.5)
    inp['w_uv'] = nrm((DEPTH, MLA_KV_LORA, MLA_HEADS * MLA_V), MLA_KV_LORA ** -0.5)
    inp['g_kn'] = gain(MLA_QK)
    inp['w_mla_out'] = nrm((DEPTH, MLA_HEADS * MLA_V, d), (MLA_HEADS * MLA_V) ** -0.5)
    inp['w_out'] = nrm((DEPTH, d, d), d ** -0.5)
    inp['g_ffn'] = gain(d)
    inp['w_gate'] = nrm((DEPTH, d, D_FF), d ** -0.5)
    inp['w_up'] = nrm((DEPTH, d, D_FF), d ** -0.5)
    inp['w_down'] = nrm((DEPTH, D_FF, d), D_FF ** -0.5)
    inp['g_ple'] = gain(d)
    inp['w_ple_gate'] = nrm((DEPTH, d, d), d ** -0.5)
    inp['w_ple_proj'] = nrm((DEPTH, PLE_DIM, d), PLE_DIM ** -0.5)
    return inp


def reference(x_prompt, x_sample, p_prompt, p_sample, cache_ckv, cache_kpe, state_ret, state_conv, page_table,
              g_mix, w_in, w_dw, b_dw, g_cln, b_cln, w_conv_out, g_rgn, w_ret_out, g_qa, w_uq, g_qn, g_kva,
              w_uk, w_uv, g_kn, w_mla_out, w_out, g_ffn, w_gate, w_up, w_down, g_ple, w_ple_gate, w_ple_proj):
    W = dict(g_mix=g_mix, w_in=w_in, w_dw=w_dw, b_dw=b_dw, g_cln=g_cln, b_cln=b_cln, w_conv_out=w_conv_out,
             g_rgn=g_rgn, w_ret_out=w_ret_out, g_qa=g_qa, w_uq=w_uq, g_qn=g_qn, g_kva=g_kva, w_uk=w_uk,
             w_uv=w_uv, g_kn=g_kn, w_mla_out=w_mla_out, w_out=w_out, g_ffn=g_ffn, w_gate=w_gate, w_up=w_up,
             w_down=w_down, g_ple=g_ple, w_ple_gate=w_ple_gate, w_ple_proj=w_ple_proj)
    Bp, Tp, _ = x_prompt.shape
    Bs, Ts, _ = x_sample.shape
    past_len = page_table.shape[1] * cache_ckv.shape[2]
    pos_p = jnp.arange(Tp, dtype=jnp.int32)
    pos_s = past_len + jnp.arange(Ts, dtype=jnp.int32)
    conv_zero = jnp.zeros((Bp, CONV_STATE, CONV_DIM), x_prompt.dtype)

    xp, xs = x_prompt, x_sample
    ckv_p, kpe_p, ret_p, conv_p = [], [], [], []
    ckv_s, kpe_s, ret_s, conv_s = [], [], [], []
    for l in range(DEPTH):
        xp, a, b, c, e = decoder_layer(xp, p_prompt[l], pos_p, conv_zero, None, None, None, W, l, True)
        ckv_p.append(a); kpe_p.append(b); ret_p.append(c); conv_p.append(e)
        past_ckv = cache_ckv[l][page_table].reshape(Bs, past_len, MLA_KV_LORA)
        past_kpe = cache_kpe[l][page_table].reshape(Bs, past_len, MLA_ROPE)
        xs, a, b, c, e = decoder_layer(xs, p_sample[l], pos_s, state_conv[l], state_ret[l], past_ckv, past_kpe, W, l, False)
        ckv_s.append(a); kpe_s.append(b); ret_s.append(c); conv_s.append(e)

    return (xp, xs, jnp.stack(ckv_p), jnp.stack(kpe_p), jnp.stack(ckv_s), jnp.stack(kpe_s),
            jnp.stack(ret_p), jnp.stack(ret_s), jnp.stack(conv_p), jnp.stack(conv_s))
```

```python
import functools

import jax
import jax.numpy as jnp
import numpy as np
from jax import lax
from jax.experimental import pallas as pl
from jax.experimental.pallas import tpu as pltpu

f32 = jnp.float32
bf16 = jnp.bfloat16

EPS = 1e-6
ROPE_BASE = 10000.0
RET_HEADS = 4
RET_DK = 128
RET_DV = 256
RET_CHUNK = 128
MLA_HEADS = 8
MLA_NOPE = 64
MLA_ROPE = 32
MLA_V = 64
MLA_QK = MLA_NOPE + MLA_ROPE
LANE = 128
HALO_ROWS = 32
VMEM_LIMIT = 56 * 1024 * 1024
MASK_VALUE = -1e30
ROW_TILE = 256
ATT_TILE = 512
CONV_TILE = 256
PAGES_PER_STEP = 16

_NT = (((1,), (1,)), ((), ()))
_TN = (((0,), (0,)), ((), ()))


def _rms(x, g):
    return x * lax.rsqrt(jnp.mean(x * x, axis=-1, keepdims=True) + EPS) * g


def _const_spec(shape, index):
    return pl.BlockSpec(shape, index, pipeline_mode=pl.Buffered(1))


def _params(sem):
    return pltpu.CompilerParams(dimension_semantics=sem, vmem_limit_bytes=VMEM_LIMIT)


def _mla_rope(x, cm, s1, s2):
    return x * cm + pltpu.roll(x, MLA_ROPE // 2, 1) * s1 + pltpu.roll(x, LANE - MLA_ROPE // 2, 1) * s2


def _in_proj_kernel(x_ref, c2r_ref, s2r_ref, cm_ref, s1m_ref, s2m_ref, gmix_ref, win_ref, gqa_ref, wuq_ref,
                    gqn_ref, gkva_ref, wuk_ref, wuv_ref, gkn_ref,
                    u_ref, q_ref, k_ref, v_ref, srg_ref, qm_ref, km_ref, vm_ref, ckv_ref, kpe_ref, gates_ref,
                    *, offs):
    o_glu, o_rq, o_rk, o_rv, o_rg, o_qa, o_kva, o_gates, o_kr, o_end = offs
    h = _rms(x_ref[...], gmix_ref[...]).astype(bf16)

    def proj(c0, c1):
        return jnp.dot(h, win_ref[:, c0:c1], preferred_element_type=f32)

    glu = proj(o_glu, o_rq)
    half = (o_rq - o_glu) // 2
    u_ref[...] = glu[:, :half] * jax.nn.sigmoid(glu[:, half:])

    c2, s2 = c2r_ref[...], s2r_ref[...]
    rq = proj(o_rq, o_rk)
    rk = proj(o_rk, o_rv)
    for hh in range(RET_HEADS):
        sl = slice(hh * RET_DK, (hh + 1) * RET_DK)
        xq, xk = rq[:, sl], rk[:, sl]
        q_ref[:, sl] = (xq * c2 + pltpu.roll(xq, RET_DK // 2, 1) * s2).astype(bf16)
        k_ref[:, sl] = ((xk * c2 + pltpu.roll(xk, RET_DK // 2, 1) * s2) * (RET_DK ** -0.5)).astype(bf16)
    v_ref[...] = proj(o_rv, o_rg).astype(bf16)
    srg_ref[...] = jax.nn.silu(proj(o_rg, o_qa)).astype(bf16)

    cm, s1, s2m = cm_ref[...], s1m_ref[...], s2m_ref[...]
    c_q = _rms(proj(o_qa, o_kva), gqa_ref[...]).astype(bf16)
    qm = jnp.dot(c_q, wuq_ref[...], preferred_element_type=f32)
    gqn = gqn_ref[...]
    for hh in range(MLA_HEADS):
        sl = slice(hh * LANE, (hh + 1) * LANE)
        r = _mla_rope(qm[:, sl], cm, s1, s2m)
        ss = jnp.sum(r * r, axis=-1, keepdims=True)
        qm_ref[:, sl] = (r * lax.rsqrt(ss / MLA_QK + EPS) * gqn).astype(bf16)

    ckv = _rms(proj(o_kva, o_gates), gkva_ref[...])
    ckv_ref[...] = ckv
    cb = ckv.astype(bf16)
    kpe_slab = _mla_rope(proj(o_kr, o_end), cm, s1, s2m)
    kpe_ref[...] = kpe_slab[:, MLA_NOPE:MLA_QK]
    kn = jnp.dot(cb, wuk_ref[...], preferred_element_type=f32)
    gkn = gkn_ref[...]
    for hh in range(MLA_HEADS):
        sl = slice(hh * LANE, (hh + 1) * LANE)
        kk = kn[:, sl] + kpe_slab
        ss = jnp.sum(kk * kk, axis=-1, keepdims=True)
        km_ref[:, sl] = (kk * lax.rsqrt(ss / MLA_QK + EPS) * gkn).astype(bf16)
    vm_ref[...] = jnp.dot(cb, wuv_ref[...], preferred_element_type=f32).astype(bf16)

    gates_ref[...] = jax.nn.sigmoid(proj(o_gates, o_kr)).astype(bf16)


def _in_proj(x2d, tabs, l, W, offs, tm, n_pos_blocks):
    M, D = x2d.shape
    conv_dim = (offs[1] - offs[0]) // 2
    n_gates = offs[8] - offs[7]
    q_lora = offs[6] - offs[5]
    kv_lora = offs[7] - offs[6]
    rows = lambda n: pl.BlockSpec((tm, n), lambda i: (i, 0))
    tab = pl.BlockSpec((tm, LANE), lambda i: (i % n_pos_blocks, 0))
    wl = lambda a: _const_spec((None,) + a.shape[1:], lambda i: (l,) + (0,) * (a.ndim - 1))
    out_dims = [(conv_dim, f32), (RET_HEADS * RET_DK, bf16), (RET_HEADS * RET_DK, bf16), (RET_HEADS * RET_DV, bf16),
                (RET_HEADS * RET_DV, bf16), (MLA_HEADS * LANE, bf16), (MLA_HEADS * LANE, bf16),
                (MLA_HEADS * MLA_V, bf16), (kv_lora, f32), (MLA_ROPE, f32), (n_gates, bf16)]
    weights = [W['g_mix'], W['w_in_p'], W['g_qa'], W['w_uq_p'], W['g_qn_p'], W['g_kva'], W['w_uk_p'], W['w_uv_b'],
               W['g_kn_p']]
    return pl.pallas_call(
        functools.partial(_in_proj_kernel, offs=offs),
        grid=(M // tm,),
        in_specs=[rows(D)] + [tab] * 5 + [wl(a) for a in weights],
        out_specs=[rows(n) for n, _ in out_dims],
        out_shape=[jax.ShapeDtypeStruct((M, n), dt) for n, dt in out_dims],
        compiler_params=_params(("parallel",)),
        name="in_proj",
    )(x2d, *tabs, *weights)


def _ln_swish(y, g, b):
    mu = jnp.mean(y, axis=-1, keepdims=True)
    d = y - mu
    var = jnp.mean(d * d, axis=-1, keepdims=True)
    z = d * lax.rsqrt(var + EPS) * g + b
    return z * jax.nn.sigmoid(z)


def _conv_prompt_kernel(prev_ref, cur_ref, w_ref, b_ref, g_ref, bl_ref, o_ref, ubuf, *, tc, width, rc):
    t = pl.program_id(1)
    ubuf[0:HALO_ROWS, :] = jnp.where(t > 0, prev_ref[0], 0.0)
    ubuf[HALO_ROWS:, :] = cur_ref[0]
    first = HALO_ROWS - (width - 1)
    C = cur_ref.shape[-1]
    for r0 in range(0, tc, rc):
        cols = []
        for c0 in range(0, C, LANE):
            acc = jnp.zeros((rc, LANE), f32) + b_ref[:, c0:c0 + LANE]
            for j in range(width):
                acc = acc + w_ref[j:j + 1, c0:c0 + LANE] * ubuf[first + r0 + j:first + r0 + j + rc, c0:c0 + LANE]
            cols.append(acc)
        y = jnp.concatenate(cols, axis=-1)
        o_ref[0, r0:r0 + rc, :] = _ln_swish(y, g_ref[...], bl_ref[...]).astype(bf16)


def _conv_prompt(u3, l, W, tc):
    B, T, C = u3.shape
    width = W['w_dw'].shape[1]
    hb = tc // HALO_ROWS
    wl = lambda a: _const_spec((None,) + a.shape[1:], lambda b, t: (l,) + (0,) * (a.ndim - 1))
    weights = [W['w_dw'], W['b_dw'], W['g_cln'], W['b_cln']]
    return pl.pallas_call(
        functools.partial(_conv_prompt_kernel, tc=tc, width=width, rc=min(64, tc)),
        grid=(B, T // tc),
        in_specs=[pl.BlockSpec((1, HALO_ROWS, C), lambda b, t: (b, jnp.maximum(t * hb - 1, 0), 0)),
                  pl.BlockSpec((1, tc, C), lambda b, t: (b, t, 0))] + [wl(a) for a in weights],
        out_specs=pl.BlockSpec((1, tc, C), lambda b, t: (b, t, 0)),
        out_shape=jax.ShapeDtypeStruct((B, T, C), bf16),
        scratch_shapes=[pltpu.VMEM((HALO_ROWS + tc, C), f32)],
        compiler_params=_params(("parallel", "parallel")),
        name="conv_prompt",
    )(u3, u3, *weights)


def _conv_sample_kernel(st_ref, u_ref, w_ref, b_ref, g_ref, bl_ref, o_ref, *, width):
    y = u_ref[...] * w_ref[width - 1:width, :] + b_ref[...]
    for j in range(width - 1):
        y = y + st_ref[:, j, :] * w_ref[j:j + 1, :]
    o_ref[...] = _ln_swish(y, g_ref[...], bl_ref[...]).astype(bf16)


def _conv_sample(state, u2, l, W):
    Bs, C = u2.shape
    width = W['w_dw'].shape[1]
    wl = lambda a: pl.BlockSpec((None,) + a.shape[1:], lambda i: (l,) + (0,) * (a.ndim - 1))
    weights = [W['w_dw'], W['b_dw'], W['g_cln'], W['b_cln']]
    return pl.pallas_call(
        functools.partial(_conv_sample_kernel, width=width),
        grid=(1,),
        in_specs=[pl.BlockSpec((None,) + state.shape[1:], lambda i: (l, 0, 0, 0)),
                  pl.BlockSpec(u2.shape, lambda i: (0, 0))] + [wl(a) for a in weights],
        out_specs=pl.BlockSpec((Bs, C), lambda i: (0, 0)),
        out_shape=jax.ShapeDtypeStruct((Bs, C), bf16),
        compiler_params=_params(("arbitrary",)),
        name="conv_sample",
    )(state, u2, *weights)


def _group_norm_gate(o, g, gate):
    mu = jnp.mean(o, axis=-1, keepdims=True)
    d = o - mu
    var = jnp.mean(d * d, axis=-1, keepdims=True)
    return (d * lax.rsqrt(var + EPS) * g * gate).astype(bf16)


def _ret_prompt_kernel(gc_ref, q_ref, k_ref, v_ref, srg_ref, decay_ref, cross_ref, kdec_ref, g_ref,
                       o_ref, s_out_ref, s_ref):
    c = pl.program_id(1)

    @pl.when(c == 0)
    def _():
        s_ref[...] = jnp.zeros_like(s_ref)

    for hh in range(RET_HEADS):
        ks = slice(hh * RET_DK, (hh + 1) * RET_DK)
        vs = slice(hh * RET_DV, (hh + 1) * RET_DV)
        q, k, v = q_ref[0, :, ks], k_ref[0, :, ks], v_ref[0, :, vs]
        s_old = s_ref[hh]
        scores = lax.dot_general(q, k, _NT, preferred_element_type=f32) * decay_ref[hh]
        o = jnp.dot(scores.astype(bf16), v, preferred_element_type=f32)
        qs = jnp.dot(q, s_old.astype(bf16), preferred_element_type=f32)
        cr = cross_ref[hh]
        o = o + qs * jnp.concatenate([cr] * (RET_DV // LANE), axis=-1)
        kd = (k.astype(f32) * kdec_ref[hh]).astype(bf16)
        s_ref[hh] = gc_ref[hh] * s_old + lax.dot_general(kd, v, _TN, preferred_element_type=f32)
        o_ref[0, :, vs] = _group_norm_gate(o, g_ref[:, vs], srg_ref[0, :, vs].astype(f32))

    @pl.when(c == pl.num_programs(1) - 1)
    def _():
        s_out_ref[0] = s_ref[...]


def _ret_prompt(q3, k3, v3, srg3, l, W, tabs):
    B, T, _ = q3.shape
    gc, decay, cross, kdec = tabs
    C = RET_CHUNK
    blk = lambda n: pl.BlockSpec((1, C, n), lambda b, c: (b, c, 0))
    tab = _const_spec((RET_HEADS, C, LANE), lambda b, c: (0, 0, 0))
    return pl.pallas_call(
        _ret_prompt_kernel,
        grid=(B, T // C),
        in_specs=[pl.BlockSpec(memory_space=pltpu.SMEM), blk(RET_HEADS * RET_DK), blk(RET_HEADS * RET_DK),
                  blk(RET_HEADS * RET_DV), blk(RET_HEADS * RET_DV), tab, tab, tab,
                  _const_spec((None, 1, RET_HEADS * RET_DV), lambda b, c: (l, 0, 0))],
        out_specs=[blk(RET_HEADS * RET_DV),
                   pl.BlockSpec((1, RET_HEADS, RET_DK, RET_DV), lambda b, c: (b, 0, 0, 0))],
        out_shape=[jax.ShapeDtypeStruct((B, T, RET_HEADS * RET_DV), bf16),
                   jax.ShapeDtypeStruct((B, RET_HEADS, RET_DK, RET_DV), f32)],
        scratch_shapes=[pltpu.VMEM((RET_HEADS, RET_DK, RET_DV), f32)],
        compiler_params=_params(("parallel", "arbitrary")),
        name="ret_prompt",
    )(gc, q3, k3, v3, srg3, decay, cross, kdec, W['g_rgn'])


def _row_to_col(row, n):
    eye = lax.broadcasted_iota(jnp.int32, (n, n), 0) == lax.broadcasted_iota(jnp.int32, (n, n), 1)
    return jnp.sum(jnp.where(eye, jnp.broadcast_to(row, (n, n)), 0.0), axis=-1, keepdims=True)


def _ret_sample_kernel(g1_ref, q_ref, k_ref, v_ref, srg_ref, s_ref, g_ref, o_ref, s_out_ref):
    for hh in range(RET_HEADS):
        ks = slice(hh * RET_DK, (hh + 1) * RET_DK)
        vs = slice(hh * RET_DV, (hh + 1) * RET_DV)
        qcol = _row_to_col(q_ref[0, :, ks].astype(f32), RET_DK)
        kcol = _row_to_col(k_ref[0, :, ks].astype(f32), RET_DK)
        v = v_ref[0, :, vs].astype(f32)
        s_new = g1_ref[hh] * s_ref[0, hh] + kcol * v
        s_out_ref[0, hh] = s_new
        o = jnp.sum(qcol * s_new, axis=0, keepdims=True)
        o_ref[0, :, vs] = _group_norm_gate(o, g_ref[:, vs], srg_ref[0, :, vs].astype(f32))


def _ret_sample(q2, k2, v2, srg2, state, l, W, g1):
    Bs = q2.shape[0]
    r3 = lambda a: a.reshape(Bs, 1, a.shape[-1])
    row = lambda n: pl.BlockSpec((1, 1, n), lambda b: (b, 0, 0))
    sblk = (1, RET_HEADS, RET_DK, RET_DV)
    o, s_new = pl.pallas_call(
        _ret_sample_kernel,
        grid=(Bs,),
        in_specs=[pl.BlockSpec(memory_space=pltpu.SMEM), row(RET_HEADS * RET_DK), row(RET_HEADS * RET_DK),
                  row(RET_HEADS * RET_DV), row(RET_HEADS * RET_DV),
                  pl.BlockSpec((None,) + sblk, lambda b: (l, b, 0, 0, 0)),
                  pl.BlockSpec((None, 1, RET_HEADS * RET_DV), lambda b: (l, 0, 0))],
        out_specs=[row(RET_HEADS * RET_DV), pl.BlockSpec(sblk, lambda b: (b, 0, 0, 0))],
        out_shape=[jax.ShapeDtypeStruct((Bs, 1, RET_HEADS * RET_DV), bf16),
                   jax.ShapeDtypeStruct((Bs, RET_HEADS, RET_DK, RET_DV), f32)],
        compiler_params=_params(("parallel",)),
        name="ret_sample",
    )(g1, r3(q2), r3(k2), r3(v2), r3(srg2), state, W['g_rgn'])
    return o.reshape(Bs, -1), s_new


def _att_prompt_kernel(qi_ref, ki_ref, q_ref, k_ref, v_ref, o_ref, m_ref, l_ref, acc_ref):
    s_idx = pl.program_id(2)
    qi, ki = qi_ref[s_idx], ki_ref[s_idx]
    tq, tk = q_ref.shape[1], k_ref.shape[1]

    @pl.when(ki == 0)
    def _():
        m_ref[...] = jnp.full_like(m_ref, MASK_VALUE)
        l_ref[...] = jnp.zeros_like(l_ref)
        acc_ref[...] = jnp.zeros_like(acc_ref)

    def step(diagonal):
        v = v_ref[0]
        for j in range(2):
            sl = slice(j * LANE, (j + 1) * LANE)
            s = lax.dot_general(q_ref[0, :, sl], k_ref[0, :, sl], _NT, preferred_element_type=f32)
            if diagonal:
                row = lax.broadcasted_iota(jnp.int32, (tq, tk), 0)
                col = lax.broadcasted_iota(jnp.int32, (tq, tk), 1)
                s = jnp.where(col <= row, s, MASK_VALUE)
            m_prev = m_ref[j]
            m_new = jnp.maximum(m_prev, jnp.max(s, axis=-1, keepdims=True))
            a = jnp.exp(m_prev - m_new)
            p = jnp.exp(s - m_new)
            l_ref[j] = a * l_ref[j] + jnp.sum(p, axis=-1, keepdims=True)
            acc_ref[j] = a * acc_ref[j] + jnp.dot(p.astype(bf16), v, preferred_element_type=f32)
            m_ref[j] = m_new

    @pl.when(ki < qi)
    def _():
        step(False)

    @pl.when(ki == qi)
    def _():
        step(True)
        lane = lax.broadcasted_iota(jnp.int32, (tq, LANE), 1)
        o = jnp.where(lane < MLA_V, acc_ref[0] / l_ref[0], acc_ref[1] / l_ref[1])
        o_ref[0] = o.astype(bf16)


def _att_prompt(qm3, km3, vm3, tq):
    B, T, _ = qm3.shape
    nq = T // tq
    pairs = [(qi, ki) for qi in range(nq) for ki in range(qi + 1)]
    qi_tab = jnp.asarray([p[0] for p in pairs], jnp.int32)
    ki_tab = jnp.asarray([p[1] for p in pairs], jnp.int32)
    grid_spec = pltpu.PrefetchScalarGridSpec(
        num_scalar_prefetch=2,
        grid=(B, MLA_HEADS // 2, len(pairs)),
        in_specs=[pl.BlockSpec((1, tq, 2 * LANE), lambda b, g, s, qi, ki: (b, qi[s], g)),
                  pl.BlockSpec((1, tq, 2 * LANE), lambda b, g, s, qi, ki: (b, ki[s], g)),
                  pl.BlockSpec((1, tq, LANE), lambda b, g, s, qi, ki: (b, ki[s], g))],
        out_specs=pl.BlockSpec((1, tq, LANE), lambda b, g, s, qi, ki: (b, qi[s], g)),
        scratch_shapes=[pltpu.VMEM((2, tq, 1), f32), pltpu.VMEM((2, tq, 1), f32), pltpu.VMEM((2, tq, LANE), f32)],
    )
    return pl.pallas_call(
        _att_prompt_kernel,
        grid_spec=grid_spec,
        out_shape=jax.ShapeDtypeStruct((B, T, MLA_HEADS * MLA_V), bf16),
        compiler_params=_params(("parallel", "parallel", "arbitrary")),
        name="att_prompt",
    )(qi_tab, ki_tab, qm3, km3, vm3)


def _hi_lo(x):
    hi = x.astype(bf16).astype(f32)
    return jnp.concatenate([hi, x - hi], axis=0).astype(bf16)


def _rows_select(rows, n=None):
    n, w = n or len(rows), rows[0].shape[-1]
    sub = lax.broadcasted_iota(jnp.int32, (n, w), 0)
    out = jnp.zeros((n, w), rows[0].dtype)
    for i, r in enumerate(rows):
        out = jnp.where(sub == i, jnp.broadcast_to(r, (n, w)), out)
    return out


def _att_sample_kernel(pt_ref, *refs, pp):
    ckv_pages, kpe_pages = refs[:pp], refs[pp:2 * pp]
    (qm_ref, kmn_ref, ckvn_ref, gkn_ref, wukt_ref, wuktp_ref, wuv_ref, o_ref,
     aq_ref, qr_ref, cb_ref, m_ref, l_ref, ctx_ref) = refs[2 * pp:]
    j = pl.program_id(1)
    H = MLA_HEADS
    n_lat = ckvn_ref.shape[-1]
    page = ckv_pages[0].shape[0]

    @pl.when(j == 0)
    def _():
        qg = qm_ref[0].astype(f32) * jnp.concatenate([gkn_ref[...]] * H, axis=-1)
        sub = lax.broadcasted_iota(jnp.int32, (2 * H, H * LANE), 0)
        head = lax.broadcasted_iota(jnp.int32, (2 * H, H * LANE), 1) // LANE
        qbd = jnp.where(sub == head, jnp.broadcast_to(qg, (2 * H, H * LANE)), 0.0).astype(bf16)
        qabs = jnp.dot(qbd, wuktp_ref[...], preferred_element_type=f32)[0:H]
        aq_ref[0:H * MLA_NOPE, :] = wukt_ref[...]
        aq_ref[H * MLA_NOPE:, :] = _hi_lo(qabs)
        qr_ref[...] = _rows_select([qg[:, h * LANE + MLA_NOPE:h * LANE + MLA_QK] for h in range(H)],
                                   2 * H).astype(bf16)
        m_ref[...] = jnp.full_like(m_ref, MASK_VALUE)
        l_ref[...] = jnp.zeros_like(l_ref)
        ctx_ref[...] = jnp.zeros_like(ctx_ref)

    for i in range(pp):
        cb_ref[i * page:(i + 1) * page, :] = ckv_pages[i][...].astype(bf16)
    kp = jnp.concatenate([kpe_pages[i][...] for i in range(pp)], axis=0)
    cb = cb_ref[...]
    r = lax.dot_general(aq_ref[...], cb, _NT, preferred_element_type=f32)
    ss = jnp.concatenate(
        [jnp.sum(jnp.square(r[h * MLA_NOPE:(h + 1) * MLA_NOPE]), axis=0, keepdims=True) for h in range(H)], axis=0)
    s_nope = r[H * MLA_NOPE:H * MLA_NOPE + H] + r[H * MLA_NOPE + H:]
    s_rope = lax.dot_general(qr_ref[...], kp.astype(bf16), _NT, preferred_element_type=f32)[0:H]
    ss_rope = lax.dot_general(jnp.ones((2 * H, MLA_ROPE), bf16), (kp * kp).astype(bf16), _NT,
                              preferred_element_type=f32)[0:H]
    s = (s_nope + s_rope) * lax.rsqrt((ss + ss_rope) / MLA_QK + EPS)
    m_prev = m_ref[...]
    m_new = jnp.maximum(m_prev, jnp.max(s, axis=-1, keepdims=True))
    a = jnp.exp(m_prev - m_new)
    p = jnp.exp(s - m_new)
    l_ref[...] = a * l_ref[...] + jnp.sum(p, axis=-1, keepdims=True)
    p16 = jnp.concatenate([p, jnp.zeros_like(p)], axis=0).astype(bf16)
    ctx_ref[...] = a * ctx_ref[...] + jnp.dot(p16, cb, preferred_element_type=f32)[0:H]
    m_ref[...] = m_new

    @pl.when(j == pl.num_programs(1) - 1)
    def _():
        prod = qm_ref[0].astype(f32) * kmn_ref[0].astype(f32)
        s_new = jnp.sum(_rows_select([prod[:, h * LANE:(h + 1) * LANE] for h in range(H)]), axis=-1, keepdims=True)
        m_prev = m_ref[...]
        m_fin = jnp.maximum(m_prev, s_new)
        a = jnp.exp(m_prev - m_fin)
        pn = jnp.exp(s_new - m_fin)
        l_fin = a * l_ref[...] + pn
        c_new = ckvn_ref[0].astype(bf16).astype(f32)
        ctx = (a * ctx_ref[...] + pn.astype(bf16).astype(f32) * c_new) / l_fin
        both = jnp.dot(_hi_lo(ctx), wuv_ref[...], preferred_element_type=f32)
        full = both[0:H] + both[H:]
        sub = lax.broadcasted_iota(jnp.int32, full.shape, 0)
        head = lax.broadcasted_iota(jnp.int32, full.shape, 1) // MLA_V
        o_ref[0] = jnp.sum(jnp.where(sub == head, full, 0.0), axis=0, keepdims=True).astype(bf16)


def _att_sample(qm2, kmn2, ckvn2, cache_ckv, cache_kpe, page_table, l, W, pp):
    Bs = qm2.shape[0]
    n_pages = page_table.shape[1]
    page, n_lat = cache_ckv.shape[2], cache_ckv.shape[3]
    rope = cache_kpe.shape[3]
    H = MLA_HEADS
    r3 = lambda a: a.reshape(Bs, 1, a.shape[-1])
    row = lambda n: pl.BlockSpec((1, 1, n), lambda b, j, pt: (b, 0, 0))
    wl = lambda a: _const_spec((None,) + a.shape[1:], lambda b, j, pt: (l,) + (0,) * (a.ndim - 1))

    def page_spec(width, i):
        return pl.BlockSpec((None, None, page, width), lambda b, j, pt: (l, pt[b, j * pp + i], 0, 0))

    weights = [W['g_kn_p'], W['w_ukt_b'], W['w_ukt_p'], W['w_uv_b']]
    grid_spec = pltpu.PrefetchScalarGridSpec(
        num_scalar_prefetch=1,
        grid=(Bs, n_pages // pp),
        in_specs=([page_spec(n_lat, i) for i in range(pp)] + [page_spec(rope, i) for i in range(pp)]
                  + [row(H * LANE), row(H * LANE), row(n_lat)] + [wl(a) for a in weights]),
        out_specs=row(H * MLA_V),
        scratch_shapes=[pltpu.VMEM((H * MLA_NOPE + 2 * H, n_lat), bf16), pltpu.VMEM((2 * H, rope), bf16),
                        pltpu.VMEM((pp * page, n_lat), bf16),
                        pltpu.VMEM((H, 1), f32), pltpu.VMEM((H, 1), f32), pltpu.VMEM((H, n_lat), f32)],
    )
    o = pl.pallas_call(
        functools.partial(_att_sample_kernel, pp=pp),
        grid_spec=grid_spec,
        out_shape=jax.ShapeDtypeStruct((Bs, 1, H * MLA_V), bf16),
        compiler_params=_params(("parallel", "arbitrary")),
        name="att_sample",
    )(page_table, *([cache_ckv] * pp), *([cache_kpe] * pp), r3(qm2), r3(kmn2), r3(ckvn2), *weights)
    return o.reshape(Bs, -1)


def _out_kernel(x_ref, ca_ref, ra_ref, om_ref, gates_ref, p_ref, wco_ref, wro_ref, wmo_ref, wout_ref, gffn_ref,
                wg_ref, wu_ref, wd_ref, gple_ref, wpg_ref, wpp_ref, o_ref):
    D = x_ref.shape[-1]
    dot = lambda a, b: jnp.dot(a, b, preferred_element_type=f32)
    gate = lambda i: gates_ref[:, i * D:(i + 1) * D].astype(f32)
    mix = (gate(0) * dot(ca_ref[...], wco_ref[...]) + gate(1) * dot(ra_ref[...], wro_ref[...])
           + gate(2) * dot(om_ref[...], wmo_ref[...]))
    x = x_ref[...] + dot(mix.astype(bf16), wout_ref[...])
    hf = _rms(x, gffn_ref[...]).astype(bf16)
    ff = jax.nn.silu(dot(hf, wg_ref[...])) * dot(hf, wu_ref[...])
    x = x + dot(ff.astype(bf16), wd_ref[...])
    hp = _rms(x, gple_ref[...]).astype(bf16)
    pg = jax.nn.sigmoid(dot(hp, wpg_ref[...]))
    o_ref[...] = x + pg * dot(p_ref[...].astype(bf16), wpp_ref[...])


def _out_stage(x2d, ca, ra, om, gates, p3, l, W, tm):
    M, D = x2d.shape
    rows = lambda n: pl.BlockSpec((tm, n), lambda i: (i, 0))
    wl = lambda a: _const_spec((None,) + a.shape[1:], lambda i: (l,) + (0,) * (a.ndim - 1))
    weights = [W['w_conv_out_b'], W['w_ret_out_b'], W['w_mla_out_b'], W['w_out_b'], W['g_ffn'], W['w_gate_b'],
               W['w_up_b'], W['w_down_b'], W['g_ple'], W['w_ple_gate_b'], W['w_ple_proj_b']]
    return pl.pallas_call(
        _out_kernel,
        grid=(M // tm,),
        in_specs=[rows(D), rows(ca.shape[1]), rows(ra.shape[1]), rows(om.shape[1]), rows(gates.shape[1]),
                  pl.BlockSpec((None, tm, p3.shape[-1]), lambda i: (l, i, 0))] + [wl(a) for a in weights],
        out_specs=rows(D),
        out_shape=jax.ShapeDtypeStruct((M, D), f32),
        compiler_params=_params(("parallel",)),
        name="out_stage",
    )(x2d, ca, ra, om, gates, p3, *weights)


def _rope_tables(pos):
    n = pos.shape[0]
    posf = pos.astype(f32)[:, None]

    def cs(half):
        inv = ROPE_BASE ** (-jnp.arange(half, dtype=f32) / half)
        ang = posf * inv[None, :]
        return jnp.cos(ang), jnp.sin(ang)

    c, s = cs(RET_DK // 2)
    c2r = jnp.concatenate([c, c], axis=-1)
    s2r = jnp.concatenate([-s, s], axis=-1)
    c, s = cs(MLA_ROPE // 2)
    hr = MLA_ROPE // 2
    z = lambda w: jnp.zeros((n, w), f32)
    cm = jnp.concatenate([jnp.ones((n, MLA_NOPE), f32), c, c, z(LANE - MLA_QK)], axis=-1)
    s1m = jnp.concatenate([z(MLA_NOPE + hr), s, z(LANE - MLA_QK)], axis=-1)
    s2m = jnp.concatenate([z(MLA_NOPE), -s, z(LANE - MLA_NOPE - hr)], axis=-1)
    return c2r, s2r, cm, s1m, s2m


def _retention_tables(C):
    log_gamma = jnp.log(1.0 - 2.0 ** (-5.0 - jnp.arange(RET_HEADS, dtype=f32)))
    idx = jnp.arange(C, dtype=f32)
    diff = idx[:, None] - idx[None, :]
    decay = jnp.where(diff >= 0, jnp.exp(jnp.maximum(diff, 0.0)[None] * log_gamma[:, None, None]), 0.0)
    cross = jnp.exp((idx + 1.0)[:, None] * log_gamma[None, :])
    kdec = jnp.exp((C - 1.0 - idx)[:, None] * log_gamma[None, :])
    gc = jnp.exp(C * log_gamma)
    lanes = lambda a: jnp.broadcast_to(a.T[:, :, None], (RET_HEADS, C, LANE))
    return gc, decay, lanes(cross), lanes(kdec)


def _pad_heads(w, heads, width):
    w = w.reshape(w.shape[:-1] + (heads, width))
    w = jnp.pad(w, [(0, 0)] * (w.ndim - 1) + [(0, LANE - width)])
    return w.reshape(w.shape[:-2] + (heads * LANE,))


def kernel(x_prompt, x_sample, p_prompt, p_sample, cache_ckv, cache_kpe, state_ret, state_conv, page_table, g_mix, w_in, w_dw, b_dw, g_cln, b_cln, w_conv_out, g_rgn, w_ret_out, g_qa, w_uq, g_qn, g_kva, w_uk, w_uv, g_kn, w_mla_out, w_out, g_ffn, w_gate, w_up, w_down, g_ple, w_ple_gate, w_ple_proj):
    B, T, D = x_prompt.shape
    Bs, Ts, _ = x_sample.shape
    assert Ts == 1, "the sample group decodes one token per request"
    depth = w_in.shape[0]
    conv_dim, width = w_dw.shape[2], w_dw.shape[1]
    q_lora, kv_lora = g_qa.shape[1], g_kva.shape[1]
    page = cache_ckv.shape[2]
    past_len = page_table.shape[1] * page
    assert T % RET_CHUNK == 0 and width - 1 <= HALO_ROWS <= T

    sizes = (2 * conv_dim, RET_HEADS * RET_DK, RET_HEADS * RET_DK, RET_HEADS * RET_DV, RET_HEADS * RET_DV,
             q_lora, kv_lora, MLA_ROPE, 3 * D)
    src = np.concatenate([[0], np.cumsum(sizes)]).tolist()
    kr_slab = jnp.pad(w_in[..., src[7]:src[8]], [(0, 0), (0, 0), (MLA_NOPE, LANE - MLA_QK)])
    w_in_p = jnp.concatenate([w_in[..., :src[7]], w_in[..., src[8]:], kr_slab], axis=-1).astype(bf16)
    offs = tuple(src[:7]) + (src[7], src[7] + 3 * D, src[7] + 3 * D + LANE)

    row = lambda g: g[:, None, :]
    lane_pad = lambda g: jnp.pad(g, [(0, 0), (0, LANE - g.shape[-1])])[:, None, :]
    W = dict(
        g_mix=row(g_mix), w_in_p=w_in_p, g_qa=row(g_qa), g_kva=row(g_kva),
        w_uq_p=_pad_heads(w_uq, MLA_HEADS, MLA_QK).astype(bf16),
        g_qn_p=lane_pad(g_qn * (MLA_QK ** -0.5)),
        w_uk_p=_pad_heads(w_uk, MLA_HEADS, MLA_NOPE).astype(bf16),
        g_kn_p=lane_pad(g_kn),
        w_uv_b=w_uv.astype(bf16),
        w_ukt_b=jnp.swapaxes(w_uk, 1, 2).astype(bf16),
        w_ukt_p=jnp.swapaxes(_pad_heads(w_uk, MLA_HEADS, MLA_NOPE), 1, 2).astype(bf16),
        w_dw=w_dw, b_dw=row(b_dw), g_cln=row(g_cln), b_cln=row(b_cln), g_rgn=row(g_rgn),
        w_conv_out_b=w_conv_out.astype(bf16), w_ret_out_b=w_ret_out.astype(bf16),
        w_mla_out_b=w_mla_out.astype(bf16), w_out_b=w_out.astype(bf16), g_ffn=row(g_ffn),
        w_gate_b=w_gate.astype(bf16), w_up_b=w_up.astype(bf16), w_down_b=w_down.astype(bf16),
        g_ple=row(g_ple), w_ple_gate_b=w_ple_gate.astype(bf16), w_ple_proj_b=w_ple_proj.astype(bf16),
    )

    tabs_p = _rope_tables(jnp.arange(T, dtype=jnp.int32))
    tabs_s = _rope_tables(jnp.full((Bs,), past_len, jnp.int32))
    ret_tabs = _retention_tables(RET_CHUNK)
    g1 = _retention_tables(1)[0]

    tm = min(ROW_TILE, T)
    tq = min(ATT_TILE, T)
    tc = min(CONV_TILE, T)
    n_pages = page_table.shape[1]
    pp = next(c for c in (PAGES_PER_STEP, 8, 4, 2, 1) if n_pages % c == 0)

    xp = x_prompt.reshape(B * T, D)
    xs = x_sample.reshape(Bs, D)
    pp3 = p_prompt.reshape(depth, B * T, -1)
    ps3 = p_sample.reshape(depth, Bs, -1)
    outs = [[] for _ in range(8)]
    for l in range(depth):
        u, q, k, v, srg, qm, km, vm, ckv, kpe, gates = _in_proj(xp, tabs_p, l, W, offs, tm, T // tm)
        s3 = lambda a: a.reshape(B, T, a.shape[-1])
        u3 = s3(u)
        ca = _conv_prompt(u3, l, W, tc).reshape(B * T, -1)
        ra, s_new = _ret_prompt(s3(q), s3(k), s3(v), s3(srg), l, W, ret_tabs)
        om = _att_prompt(s3(qm), s3(km), s3(vm), tq)
        xp = _out_stage(xp, ca, ra.reshape(B * T, -1), om.reshape(B * T, -1), gates, pp3, l, W, tm)
        outs[0].append(s3(ckv)); outs[1].append(s3(kpe)); outs[4].append(s_new)
        outs[6].append(u3[:, T - (width - 1):])

        u, q, k, v, srg, qm, km, vm, ckv, kpe, gates = _in_proj(xs, tabs_s, l, W, offs, Bs, 1)
        ca = _conv_sample(state_conv, u, l, W)
        ra, s_new = _ret_sample(q, k, v, srg, state_ret, l, W, g1)
        om = _att_sample(qm, km, ckv, cache_ckv, cache_kpe, page_table, l, W, pp)
        xs = _out_stage(xs, ca, ra, om, gates, ps3, l, W, Bs)
        outs[2].append(ckv[:, None]); outs[3].append(kpe[:, None]); outs[5].append(s_new)
        outs[7].append(jnp.concatenate([state_conv[l][:, 1:], u[:, None]], axis=1))

    st = [jnp.stack(o) for o in outs]
    return (xp.reshape(B, T, D), xs.reshape(Bs, Ts, D), st[0], st[1], st[2], st[3], st[4], st[5], st[6], st[7])
```

```python
import functools

import jax
import jax.numpy as jnp
import numpy as np
from jax import lax
from jax.experimental import pallas as pl
from jax.experimental.pallas import tpu as pltpu

f32 = jnp.float32
bf16 = jnp.bfloat16

EPS = 1e-6
ROPE_BASE = 10000.0
RET_HEADS = 4
RET_DK = 128
RET_DV = 256
RET_CHUNK = 128
MLA_HEADS = 8
MLA_NOPE = 64
MLA_ROPE = 32
MLA_V = 64
MLA_QK = MLA_NOPE + MLA_ROPE
LANE = 128
SUBLANES = 8
HALO_ROWS = 32
VMEM_LIMIT = 56 * 1024 * 1024
MASK_VALUE = -1e30
LOG2_E = 1.4426950408889634
ROW_TILE = 256
ATT_TILE = 512
CONV_TILE = 256
PAGES_PER_STEP = 16

_NT = (((1,), (1,)), ((), ()))
_TN = (((0,), (0,)), ((), ()))


def _rms(x, g):
    return x * lax.rsqrt(jnp.mean(x * x, axis=-1, keepdims=True) + EPS) * g


def _const_spec(shape, index):
    return pl.BlockSpec(shape, index, pipeline_mode=pl.Buffered(1))


def _params(sem):
    return pltpu.CompilerParams(dimension_semantics=sem, vmem_limit_bytes=VMEM_LIMIT)


def _mla_rope(x, cm, s1, s2):
    return x * cm + pltpu.roll(x, MLA_ROPE // 2, 1) * s1 + pltpu.roll(x, LANE - MLA_ROPE // 2, 1) * s2


def _in_proj_kernel(x_ref, c2r_ref, s2r_ref, cm_ref, s1m_ref, s2m_ref, gmix_ref, win_ref, gqa_ref, wuq_ref,
                    gqn_ref, gkva_ref, wuk_ref, wuv_ref, gkn_ref,
                    u_ref, q_ref, k_ref, v_ref, srg_ref, qm_ref, km_ref, vm_ref, ckv_ref, kpe_ref, gates_ref,
                    *, offs):
    o_glu, o_rq, o_rk, o_rv, o_rg, o_qa, o_kr, o_kva, o_gates, o_end = offs
    h = _rms(x_ref[...], gmix_ref[...]).astype(bf16)

    def proj(c0, c1):
        return jnp.dot(h, win_ref[:, c0:c1], preferred_element_type=f32)

    glu = proj(o_glu, o_rq)
    half = (o_rq - o_glu) // 2
    u_ref[...] = glu[:, :half] * jax.nn.sigmoid(glu[:, half:])

    c2, s2 = c2r_ref[...], s2r_ref[...]
    rq = proj(o_rq, o_rk)
    rk = proj(o_rk, o_rv)
    for hh in range(RET_HEADS):
        sl = slice(hh * RET_DK, (hh + 1) * RET_DK)
        xq, xk = rq[:, sl], rk[:, sl]
        q_ref[:, sl] = (xq * c2 + pltpu.roll(xq, RET_DK // 2, 1) * s2).astype(bf16)
        k_ref[:, sl] = ((xk * c2 + pltpu.roll(xk, RET_DK // 2, 1) * s2) * (RET_DK ** -0.5)).astype(bf16)
    v_ref[...] = proj(o_rv, o_rg).astype(bf16)
    srg_ref[...] = jax.nn.silu(proj(o_rg, o_qa)).astype(bf16)

    cm, s1, s2m = cm_ref[...], s1m_ref[...], s2m_ref[...]
    qa_kr = proj(o_qa, o_kva)
    c_q = _rms(qa_kr[:, :o_kr - o_qa], gqa_ref[...]).astype(bf16)
    qm = jnp.dot(c_q, wuq_ref[...], preferred_element_type=f32)
    gqn = gqn_ref[...]
    for hh in range(MLA_HEADS):
        sl = slice(hh * LANE, (hh + 1) * LANE)
        r = _mla_rope(qm[:, sl], cm, s1, s2m)
        ss = jnp.sum(r * r, axis=-1, keepdims=True)
        qm_ref[:, sl] = (r * lax.rsqrt(ss / MLA_QK + EPS) * gqn).astype(bf16)

    ckv = _rms(proj(o_kva, o_gates), gkva_ref[...])
    ckv_ref[...] = ckv
    cb = ckv.astype(bf16)
    kpe_slab = _mla_rope(qa_kr[:, o_kr - o_qa:], cm, s1, s2m)
    kpe_ref[...] = kpe_slab[:, MLA_NOPE:MLA_QK]
    kn = jnp.dot(cb, wuk_ref[...], preferred_element_type=f32)
    gkn = gkn_ref[...]
    for hh in range(MLA_HEADS):
        sl = slice(hh * LANE, (hh + 1) * LANE)
        kk = kn[:, sl] + kpe_slab
        ss = jnp.sum(kk * kk, axis=-1, keepdims=True)
        km_ref[:, sl] = (kk * lax.rsqrt(ss / MLA_QK + EPS) * gkn).astype(bf16)
    vm_ref[...] = jnp.dot(cb, wuv_ref[...], preferred_element_type=f32).astype(bf16)

    gates_ref[...] = jax.nn.sigmoid(proj(o_gates, o_end)).astype(bf16)


def _in_proj(x2d, tabs, l, W, offs, tm, n_pos_blocks):
    M, D = x2d.shape
    conv_dim = (offs[1] - offs[0]) // 2
    n_gates = offs[9] - offs[8]
    kv_lora = offs[8] - offs[7]
    rows = lambda n: pl.BlockSpec((tm, n), lambda i: (i, 0))
    tab = pl.BlockSpec((tm, LANE), lambda i: (i % n_pos_blocks, 0))
    wl = lambda a: _const_spec((None,) + a.shape[1:], lambda i: (l,) + (0,) * (a.ndim - 1))
    out_dims = [(conv_dim, f32), (RET_HEADS * RET_DK, bf16), (RET_HEADS * RET_DK, bf16), (RET_HEADS * RET_DV, bf16),
                (RET_HEADS * RET_DV, bf16), (MLA_HEADS * LANE, bf16), (MLA_HEADS * LANE, bf16),
                (MLA_HEADS * MLA_V, bf16), (kv_lora, f32), (MLA_ROPE, f32), (n_gates, bf16)]
    weights = [W['g_mix'], W['w_in_p'], W['g_qa'], W['w_uq_p'], W['g_qn_p'], W['g_kva'], W['w_uk_p'], W['w_uv_b'],
               W['g_kn_p']]
    return pl.pallas_call(
        functools.partial(_in_proj_kernel, offs=offs),
        grid=(M // tm,),
        in_specs=[rows(D)] + [tab] * 5 + [wl(a) for a in weights],
        out_specs=[rows(n) for n, _ in out_dims],
        out_shape=[jax.ShapeDtypeStruct((M, n), dt) for n, dt in out_dims],
        compiler_params=_params(("parallel",)),
        name="in_proj",
    )(x2d, *tabs, *weights)


def _ln_swish(y, g, b):
    mu = jnp.mean(y, axis=-1, keepdims=True)
    d = y - mu
    var = jnp.mean(d * d, axis=-1, keepdims=True)
    z = d * lax.rsqrt(var + EPS) * g + b
    return z * jax.nn.sigmoid(z)


def _conv_prompt_kernel(prev_ref, cur_ref, w_ref, b_ref, g_ref, bl_ref, o_ref, ubuf, sbuf, *, tc, width, rc):
    t = pl.program_id(1)
    ubuf[0:HALO_ROWS, :] = jnp.where(t > 0, prev_ref[0], 0.0)
    ubuf[HALO_ROWS:, :] = cur_ref[0]
    first = HALO_ROWS - (width - 1)
    C = cur_ref.shape[-1]
    R = HALO_ROWS + tc
    sbuf[0] = ubuf[...]
    for ph in range(1, SUBLANES):
        sbuf[ph, 0:R - SUBLANES, :] = ubuf[ph:R - SUBLANES + ph, :]
    for r0 in range(0, tc, rc):
        cols = []
        for c0 in range(0, C, LANE):
            acc = jnp.zeros((rc, LANE), f32) + b_ref[:, c0:c0 + LANE]
            for j in range(width):
                ph = (first + j) % SUBLANES
                base = first + j - ph + r0
                acc = acc + w_ref[j:j + 1, c0:c0 + LANE] * sbuf[ph, base:base + rc, c0:c0 + LANE]
            cols.append(acc)
        y = jnp.concatenate(cols, axis=-1)
        o_ref[0, r0:r0 + rc, :] = _ln_swish(y, g_ref[...], bl_ref[...]).astype(bf16)


def _conv_prompt(u3, l, W, tc):
    B, T, C = u3.shape
    width = W['w_dw'].shape[1]
    hb = tc // HALO_ROWS
    wl = lambda a: _const_spec((None,) + a.shape[1:], lambda b, t: (l,) + (0,) * (a.ndim - 1))
    weights = [W['w_dw'], W['b_dw'], W['g_cln'], W['b_cln']]
    return pl.pallas_call(
        functools.partial(_conv_prompt_kernel, tc=tc, width=width, rc=min(64, tc)),
        grid=(B, T // tc),
        in_specs=[pl.BlockSpec((1, HALO_ROWS, C), lambda b, t: (b, jnp.maximum(t * hb - 1, 0), 0)),
                  pl.BlockSpec((1, tc, C), lambda b, t: (b, t, 0))] + [wl(a) for a in weights],
        out_specs=pl.BlockSpec((1, tc, C), lambda b, t: (b, t, 0)),
        out_shape=jax.ShapeDtypeStruct((B, T, C), bf16),
        scratch_shapes=[pltpu.VMEM((HALO_ROWS + tc, C), f32), pltpu.VMEM((SUBLANES, HALO_ROWS + tc, C), f32)],
        compiler_params=_params(("parallel", "parallel")),
        name="conv_prompt",
    )(u3, u3, *weights)


def _conv_sample_kernel(st_ref, u_ref, w_ref, b_ref, g_ref, bl_ref, o_ref, *, width):
    y = u_ref[...] * w_ref[width - 1:width, :] + b_ref[...]
    for j in range(width - 1):
        y = y + st_ref[:, j, :] * w_ref[j:j + 1, :]
    o_ref[...] = _ln_swish(y, g_ref[...], bl_ref[...]).astype(bf16)


def _conv_sample(state, u2, l, W):
    Bs, C = u2.shape
    width = W['w_dw'].shape[1]
    wl = lambda a: pl.BlockSpec((None,) + a.shape[1:], lambda i: (l,) + (0,) * (a.ndim - 1))
    weights = [W['w_dw'], W['b_dw'], W['g_cln'], W['b_cln']]
    return pl.pallas_call(
        functools.partial(_conv_sample_kernel, width=width),
        grid=(1,),
        in_specs=[pl.BlockSpec((None,) + state.shape[1:], lambda i: (l, 0, 0, 0)),
                  pl.BlockSpec(u2.shape, lambda i: (0, 0))] + [wl(a) for a in weights],
        out_specs=pl.BlockSpec((Bs, C), lambda i: (0, 0)),
        out_shape=jax.ShapeDtypeStruct((Bs, C), bf16),
        compiler_params=_params(("arbitrary",)),
        name="conv_sample",
    )(state, u2, *weights)


def _group_norm_gate(o, g, gate):
    mu = jnp.mean(o, axis=-1, keepdims=True)
    d = o - mu
    var = jnp.mean(d * d, axis=-1, keepdims=True)
    return (d * lax.rsqrt(var + EPS) * g * gate).astype(bf16)


def _ret_prompt_kernel(gc_ref, q_ref, k_ref, v_ref, srg_ref, decay_ref, cross_ref, kdec_ref, g_ref,
                       o_ref, s_out_ref, s_ref):
    c = pl.program_id(1)

    @pl.when(c == 0)
    def _():
        s_ref[...] = jnp.zeros_like(s_ref)

    for hh in range(RET_HEADS):
        ks = slice(hh * RET_DK, (hh + 1) * RET_DK)
        vs = slice(hh * RET_DV, (hh + 1) * RET_DV)
        q, k, v = q_ref[0, :, ks], k_ref[0, :, ks], v_ref[0, :, vs]
        s_old = s_ref[hh]
        scores = lax.dot_general(q, k, _NT, preferred_element_type=f32) * decay_ref[hh]
        o = jnp.dot(scores.astype(bf16), v, preferred_element_type=f32)
        qs = jnp.dot(q, s_old.astype(bf16), preferred_element_type=f32)
        cr = cross_ref[hh]
        o = o + qs * jnp.concatenate([cr] * (RET_DV // LANE), axis=-1)
        kd = (k.astype(f32) * kdec_ref[hh]).astype(bf16)
        s_ref[hh] = gc_ref[hh] * s_old + lax.dot_general(kd, v, _TN, preferred_element_type=f32)
        o_ref[0, :, vs] = _group_norm_gate(o, g_ref[:, vs], srg_ref[0, :, vs].astype(f32))

    @pl.when(c == pl.num_programs(1) - 1)
    def _():
        s_out_ref[0] = s_ref[...]


def _ret_prompt(q3, k3, v3, srg3, l, W, tabs):
    B, T, _ = q3.shape
    gc, decay, cross, kdec = tabs
    C = RET_CHUNK
    blk = lambda n: pl.BlockSpec((1, C, n), lambda b, c: (b, c, 0))
    tab = _const_spec((RET_HEADS, C, LANE), lambda b, c: (0, 0, 0))
    return pl.pallas_call(
        _ret_prompt_kernel,
        grid=(B, T // C),
        in_specs=[pl.BlockSpec(memory_space=pltpu.SMEM), blk(RET_HEADS * RET_DK), blk(RET_HEADS * RET_DK),
                  blk(RET_HEADS * RET_DV), blk(RET_HEADS * RET_DV), tab, tab, tab,
                  _const_spec((None, 1, RET_HEADS * RET_DV), lambda b, c: (l, 0, 0))],
        out_specs=[blk(RET_HEADS * RET_DV),
                   pl.BlockSpec((1, RET_HEADS, RET_DK, RET_DV), lambda b, c: (b, 0, 0, 0))],
        out_shape=[jax.ShapeDtypeStruct((B, T, RET_HEADS * RET_DV), bf16),
                   jax.ShapeDtypeStruct((B, RET_HEADS, RET_DK, RET_DV), f32)],
        scratch_shapes=[pltpu.VMEM((RET_HEADS, RET_DK, RET_DV), f32)],
        compiler_params=_params(("parallel", "arbitrary")),
        name="ret_prompt",
    )(gc, q3, k3, v3, srg3, decay, cross, kdec, W['g_rgn'])


def _row_to_col(row, n):
    eye = lax.broadcasted_iota(jnp.int32, (n, n), 0) == lax.broadcasted_iota(jnp.int32, (n, n), 1)
    return jnp.sum(jnp.where(eye, jnp.broadcast_to(row, (n, n)), 0.0), axis=-1, keepdims=True)


def _ret_sample_kernel(g1_ref, q_ref, k_ref, v_ref, srg_ref, s_ref, g_ref, o_ref, s_out_ref):
    for hh in range(RET_HEADS):
        ks = slice(hh * RET_DK, (hh + 1) * RET_DK)
        vs = slice(hh * RET_DV, (hh + 1) * RET_DV)
        qcol = _row_to_col(q_ref[0, :, ks].astype(f32), RET_DK)
        kcol = _row_to_col(k_ref[0, :, ks].astype(f32), RET_DK)
        v = v_ref[0, :, vs].astype(f32)
        s_new = g1_ref[hh] * s_ref[0, hh] + kcol * v
        s_out_ref[0, hh] = s_new
        o = jnp.sum(qcol * s_new, axis=0, keepdims=True)
        o_ref[0, :, vs] = _group_norm_gate(o, g_ref[:, vs], srg_ref[0, :, vs].astype(f32))


def _ret_sample(q2, k2, v2, srg2, state, l, W, g1):
    Bs = q2.shape[0]
    r3 = lambda a: a.reshape(Bs, 1, a.shape[-1])
    row = lambda n: pl.BlockSpec((1, 1, n), lambda b: (b, 0, 0))
    sblk = (1, RET_HEADS, RET_DK, RET_DV)
    o, s_new = pl.pallas_call(
        _ret_sample_kernel,
        grid=(Bs,),
        in_specs=[pl.BlockSpec(memory_space=pltpu.SMEM), row(RET_HEADS * RET_DK), row(RET_HEADS * RET_DK),
                  row(RET_HEADS * RET_DV), row(RET_HEADS * RET_DV),
                  pl.BlockSpec((None,) + sblk, lambda b: (l, b, 0, 0, 0)),
                  pl.BlockSpec((None, 1, RET_HEADS * RET_DV), lambda b: (l, 0, 0))],
        out_specs=[row(RET_HEADS * RET_DV), pl.BlockSpec(sblk, lambda b: (b, 0, 0, 0))],
        out_shape=[jax.ShapeDtypeStruct((Bs, 1, RET_HEADS * RET_DV), bf16),
                   jax.ShapeDtypeStruct((Bs, RET_HEADS, RET_DK, RET_DV), f32)],
        compiler_params=_params(("parallel",)),
        name="ret_sample",
    )(g1, r3(q2), r3(k2), r3(v2), r3(srg2), state, W['g_rgn'])
    return o.reshape(Bs, -1), s_new


def _att_prompt_kernel(qi_ref, ki_ref, q_ref, k_ref, v_ref, o_ref, m_ref, l_ref, acc_ref):
    s_idx = pl.program_id(1)
    qi, ki = qi_ref[s_idx], ki_ref[s_idx]
    tq, tk = q_ref.shape[1], k_ref.shape[1]
    n_lt = tk // LANE

    @pl.when(ki == 0)
    def _():
        m_ref[...] = jnp.full_like(m_ref, MASK_VALUE)
        l_ref[...] = jnp.zeros_like(l_ref)
        acc_ref[...] = jnp.zeros_like(acc_ref)

    def step(diagonal):
        for hd in range(MLA_HEADS):
            v = v_ref[0, :, (hd // 2) * LANE:(hd // 2 + 1) * LANE]
            sl = slice(hd * LANE, (hd + 1) * LANE)
            s = lax.dot_general(q_ref[0, :, sl], k_ref[0, :, sl], _NT, preferred_element_type=f32)
            if diagonal:
                row = lax.broadcasted_iota(jnp.int32, (tq, tk), 0)
                col = lax.broadcasted_iota(jnp.int32, (tq, tk), 1)
                s = jnp.where(col <= row, s, MASK_VALUE)
            m_prev = m_ref[hd]
            smax = s[:, 0:LANE]
            for c in range(1, n_lt):
                smax = jnp.maximum(smax, s[:, c * LANE:(c + 1) * LANE])
            m_new = jnp.maximum(m_prev, jnp.max(smax, axis=-1, keepdims=True))
            a = jnp.exp2(m_prev - m_new)
            p = jnp.exp2(s - jnp.concatenate([m_new] * n_lt, axis=-1))
            psum = p[:, 0:LANE]
            for c in range(1, n_lt):
                psum = psum + p[:, c * LANE:(c + 1) * LANE]
            l_ref[hd] = a * l_ref[hd] + psum
            acc_ref[hd] = a * acc_ref[hd] + jnp.dot(p.astype(bf16), v, preferred_element_type=f32)
            m_ref[hd] = m_new

    @pl.when(ki < qi)
    def _():
        step(False)

    @pl.when(ki == qi)
    def _():
        step(True)
        lane = lax.broadcasted_iota(jnp.int32, (tq, LANE), 1)
        for g in range(MLA_HEADS // 2):
            l0 = jnp.sum(l_ref[2 * g], axis=-1, keepdims=True)
            l1 = jnp.sum(l_ref[2 * g + 1], axis=-1, keepdims=True)
            o = jnp.where(lane < MLA_V, acc_ref[2 * g] / l0, acc_ref[2 * g + 1] / l1)
            o_ref[0, :, g * LANE:(g + 1) * LANE] = o.astype(bf16)


def _att_prompt(qm3, km3, vm3, tq):
    B, T, _ = qm3.shape
    nq = T // tq
    H = MLA_HEADS
    pairs = [(qi, ki) for qi in range(nq) for ki in range(qi + 1)]
    qi_tab = jnp.asarray([p[0] for p in pairs], jnp.int32)
    ki_tab = jnp.asarray([p[1] for p in pairs], jnp.int32)
    grid_spec = pltpu.PrefetchScalarGridSpec(
        num_scalar_prefetch=2,
        grid=(B, len(pairs)),
        in_specs=[pl.BlockSpec((1, tq, H * LANE), lambda b, s, qi, ki: (b, qi[s], 0)),
                  pl.BlockSpec((1, tq, H * LANE), lambda b, s, qi, ki: (b, ki[s], 0)),
                  pl.BlockSpec((1, tq, H * MLA_V), lambda b, s, qi, ki: (b, ki[s], 0))],
        out_specs=pl.BlockSpec((1, tq, H * MLA_V), lambda b, s, qi, ki: (b, qi[s], 0)),
        scratch_shapes=[pltpu.VMEM((H, tq, LANE), f32), pltpu.VMEM((H, tq, LANE), f32),
                        pltpu.VMEM((H, tq, LANE), f32)],
    )
    return pl.pallas_call(
        _att_prompt_kernel,
        grid_spec=grid_spec,
        out_shape=jax.ShapeDtypeStruct((B, T, H * MLA_V), bf16),
        compiler_params=_params(("parallel", "arbitrary")),
        name="att_prompt",
    )(qi_tab, ki_tab, qm3, km3, vm3)


def _hi_lo(x):
    hi = x.astype(bf16).astype(f32)
    return jnp.concatenate([hi, x - hi], axis=0).astype(bf16)


def _rows_select(rows, n=None):
    n, w = n or len(rows), rows[0].shape[-1]
    sub = lax.broadcasted_iota(jnp.int32, (n, w), 0)
    out = jnp.zeros((n, w), rows[0].dtype)
    for i, r in enumerate(rows):
        out = jnp.where(sub == i, jnp.broadcast_to(r, (n, w)), out)
    return out


def _att_sample_kernel(pt_ref, *refs, pp):
    ckv_pages, kpe_pages = refs[:pp], refs[pp:2 * pp]
    (qm_ref, kmn_ref, ckvn_ref, gkn_ref, wukt_ref, wuktp_ref, wuv_ref, o_ref,
     aq_ref, qr_ref, m_ref, l_ref, ctx_ref) = refs[2 * pp:]
    j = pl.program_id(1)
    H = MLA_HEADS

    @pl.when(j == 0)
    def _():
        qg = qm_ref[0].astype(f32) * jnp.concatenate([gkn_ref[...]] * H, axis=-1)
        sub = lax.broadcasted_iota(jnp.int32, (2 * H, H * LANE), 0)
        head = lax.broadcasted_iota(jnp.int32, (2 * H, H * LANE), 1) // LANE
        qbd = jnp.where(sub == head, jnp.broadcast_to(qg, (2 * H, H * LANE)), 0.0).astype(bf16)
        qabs = jnp.dot(qbd, wuktp_ref[...], preferred_element_type=f32)[0:H]
        aq_ref[0:H * MLA_NOPE, :] = wukt_ref[...]
        aq_ref[H * MLA_NOPE:, :] = _hi_lo(qabs)
        qr_ref[...] = _rows_select([qg[:, h * LANE + MLA_NOPE:h * LANE + MLA_QK] for h in range(H)],
                                   2 * H).astype(bf16)
        m_ref[...] = jnp.full_like(m_ref, MASK_VALUE)
        l_ref[...] = jnp.zeros_like(l_ref)
        ctx_ref[...] = jnp.zeros_like(ctx_ref)

    cb = jnp.concatenate([ckv_pages[i][...].astype(bf16) for i in range(pp)], axis=0)
    kt = jnp.concatenate([kpe_pages[i][...] for i in range(pp)], axis=1)
    r = lax.dot_general(aq_ref[...], cb, _NT, preferred_element_type=f32)
    ss = jnp.concatenate(
        [jnp.sum(jnp.square(r[h * MLA_NOPE:(h + 1) * MLA_NOPE]), axis=0, keepdims=True) for h in range(H)], axis=0)
    ss_rope = jnp.sum(kt * kt, axis=0, keepdims=True)
    s_nope = r[H * MLA_NOPE:H * MLA_NOPE + H] + r[H * MLA_NOPE + H:]
    s_rope = jnp.dot(qr_ref[...], kt.astype(bf16), preferred_element_type=f32)[0:H]
    s = (s_nope + s_rope) * lax.rsqrt((ss + ss_rope) / MLA_QK + EPS)
    m_prev = m_ref[...]
    m_new = jnp.maximum(m_prev, jnp.max(s, axis=-1, keepdims=True))
    a = jnp.exp2(m_prev - m_new)
    p = jnp.exp2(s - m_new)
    l_ref[...] = a * l_ref[...] + jnp.sum(p, axis=-1, keepdims=True)
    p16 = jnp.concatenate([p, jnp.zeros_like(p)], axis=0).astype(bf16)
    ctx_ref[...] = a * ctx_ref[...] + jnp.dot(p16, cb, preferred_element_type=f32)[0:H]
    m_ref[...] = m_new

    @pl.when(j == pl.num_programs(1) - 1)
    def _():
        prod = qm_ref[0].astype(f32) * kmn_ref[0].astype(f32)
        s_new = jnp.sum(_rows_select([prod[:, h * LANE:(h + 1) * LANE] for h in range(H)]), axis=-1, keepdims=True)
        m_prev = m_ref[...]
        m_fin = jnp.maximum(m_prev, s_new)
        a = jnp.exp2(m_prev - m_fin)
        pn = jnp.exp2(s_new - m_fin)
        l_fin = a * l_ref[...] + pn
        c_new = ckvn_ref[0].astype(bf16).astype(f32)
        ctx = (a * ctx_ref[...] + pn.astype(bf16).astype(f32) * c_new) / l_fin
        both = jnp.dot(_hi_lo(ctx), wuv_ref[...], preferred_element_type=f32)
        full = both[0:H] + both[H:]
        sub = lax.broadcasted_iota(jnp.int32, full.shape, 0)
        head = lax.broadcasted_iota(jnp.int32, full.shape, 1) // MLA_V
        o_ref[0] = jnp.sum(jnp.where(sub == head, full, 0.0), axis=0, keepdims=True).astype(bf16)


def _att_sample(qm2, kmn2, ckvn2, cache_ckv, cache_kpe_t, page_table, l, W, pp):
    Bs = qm2.shape[0]
    n_pages = page_table.shape[1]
    page, n_lat = cache_ckv.shape[2], cache_ckv.shape[3]
    rope = cache_kpe_t.shape[2]
    H = MLA_HEADS
    r3 = lambda a: a.reshape(Bs, 1, a.shape[-1])
    row = lambda n: pl.BlockSpec((1, 1, n), lambda b, j, pt: (b, 0, 0))
    wl = lambda a: _const_spec((None,) + a.shape[1:], lambda b, j, pt: (l,) + (0,) * (a.ndim - 1))

    def page_spec(shape, i):
        return pl.BlockSpec((None, None) + shape, lambda b, j, pt: (l, pt[b, j * pp + i], 0, 0))

    weights = [W['g_kn_p'], W['w_ukt_b'], W['w_ukt_p'], W['w_uv_b']]
    grid_spec = pltpu.PrefetchScalarGridSpec(
        num_scalar_prefetch=1,
        grid=(Bs, n_pages // pp),
        in_specs=([page_spec((page, n_lat), i) for i in range(pp)]
                  + [page_spec((rope, page), i) for i in range(pp)]
                  + [row(H * LANE), row(H * LANE), row(n_lat)] + [wl(a) for a in weights]),
        out_specs=row(H * MLA_V),
        scratch_shapes=[pltpu.VMEM((H * MLA_NOPE + 2 * H, n_lat), bf16), pltpu.VMEM((2 * H, rope), bf16),
                        pltpu.VMEM((H, 1), f32), pltpu.VMEM((H, 1), f32), pltpu.VMEM((H, n_lat), f32)],
    )
    o = pl.pallas_call(
        functools.partial(_att_sample_kernel, pp=pp),
        grid_spec=grid_spec,
        out_shape=jax.ShapeDtypeStruct((Bs, 1, H * MLA_V), bf16),
        compiler_params=_params(("parallel", "arbitrary")),
        name="att_sample",
    )(page_table, *([cache_ckv] * pp), *([cache_kpe_t] * pp), r3(qm2), r3(kmn2), r3(ckvn2), *weights)
    return o.reshape(Bs, -1)


def _out_kernel(x_ref, ca_ref, ra_ref, om_ref, gates_ref, p_ref, wco_ref, wro_ref, wmo_ref, wout_ref, gffn_ref,
                wg_ref, wu_ref, wd_ref, gple_ref, wpg_ref, wpp_ref, o_ref):
    D = x_ref.shape[-1]
    dot = lambda a, b: jnp.dot(a, b, preferred_element_type=f32)
    gate = lambda i: gates_ref[:, i * D:(i + 1) * D].astype(f32)
    mix = (gate(0) * dot(ca_ref[...], wco_ref[...]) + gate(1) * dot(ra_ref[...], wro_ref[...])
           + gate(2) * dot(om_ref[...], wmo_ref[...]))
    x = x_ref[...] + dot(mix.astype(bf16), wout_ref[...])
    hf = _rms(x, gffn_ref[...]).astype(bf16)
    ff = jax.nn.silu(dot(hf, wg_ref[...])) * dot(hf, wu_ref[...])
    x = x + dot(ff.astype(bf16), wd_ref[...])
    hp = _rms(x, gple_ref[...]).astype(bf16)
    pg = jax.nn.sigmoid(dot(hp, wpg_ref[...]))
    o_ref[...] = x + pg * dot(p_ref[...].astype(bf16), wpp_ref[...])


def _out_stage(x2d, ca, ra, om, gates, p3, l, W, tm):
    M, D = x2d.shape
    rows = lambda n: pl.BlockSpec((tm, n), lambda i: (i, 0))
    wl = lambda a: _const_spec((None,) + a.shape[1:], lambda i: (l,) + (0,) * (a.ndim - 1))
    weights = [W['w_conv_out_b'], W['w_ret_out_b'], W['w_mla_out_b'], W['w_out_b'], W['g_ffn'], W['w_gate_b'],
               W['w_up_b'], W['w_down_b'], W['g_ple'], W['w_ple_gate_b'], W['w_ple_proj_b']]
    return pl.pallas_call(
        _out_kernel,
        grid=(M // tm,),
        in_specs=[rows(D), rows(ca.shape[1]), rows(ra.shape[1]), rows(om.shape[1]), rows(gates.shape[1]),
                  pl.BlockSpec((None, tm, p3.shape[-1]), lambda i: (l, i, 0))] + [wl(a) for a in weights],
        out_specs=rows(D),
        out_shape=jax.ShapeDtypeStruct((M, D), f32),
        compiler_params=_params(("parallel",)),
        name="out_stage",
    )(x2d, ca, ra, om, gates, p3, *weights)


def _rope_tables(pos):
    n = pos.shape[0]
    posf = pos.astype(f32)[:, None]

    def cs(half):
        inv = ROPE_BASE ** (-jnp.arange(half, dtype=f32) / half)
        ang = posf * inv[None, :]
        return jnp.cos(ang), jnp.sin(ang)

    c, s = cs(RET_DK // 2)
    c2r = jnp.concatenate([c, c], axis=-1)
    s2r = jnp.concatenate([-s, s], axis=-1)
    c, s = cs(MLA_ROPE // 2)
    hr = MLA_ROPE // 2
    z = lambda w: jnp.zeros((n, w), f32)
    cm = jnp.concatenate([jnp.ones((n, MLA_NOPE), f32), c, c, z(LANE - MLA_QK)], axis=-1)
    s1m = jnp.concatenate([z(MLA_NOPE + hr), s, z(LANE - MLA_QK)], axis=-1)
    s2m = jnp.concatenate([z(MLA_NOPE), -s, z(LANE - MLA_NOPE - hr)], axis=-1)
    return c2r, s2r, cm, s1m, s2m


def _retention_tables(C):
    log_gamma = jnp.log(1.0 - 2.0 ** (-5.0 - jnp.arange(RET_HEADS, dtype=f32)))
    idx = jnp.arange(C, dtype=f32)
    diff = idx[:, None] - idx[None, :]
    decay = jnp.where(diff >= 0, jnp.exp(jnp.maximum(diff, 0.0)[None] * log_gamma[:, None, None]), 0.0)
    cross = jnp.exp((idx + 1.0)[:, None] * log_gamma[None, :])
    kdec = jnp.exp((C - 1.0 - idx)[:, None] * log_gamma[None, :])
    gc = jnp.exp(C * log_gamma)
    lanes = lambda a: jnp.broadcast_to(a.T[:, :, None], (RET_HEADS, C, LANE))
    return gc, decay, lanes(cross), lanes(kdec)


def _pad_heads(w, heads, width):
    w = w.reshape(w.shape[:-1] + (heads, width))
    w = jnp.pad(w, [(0, 0)] * (w.ndim - 1) + [(0, LANE - width)])
    return w.reshape(w.shape[:-2] + (heads * LANE,))


def kernel(x_prompt, x_sample, p_prompt, p_sample, cache_ckv, cache_kpe, state_ret, state_conv, page_table, g_mix, w_in, w_dw, b_dw, g_cln, b_cln, w_conv_out, g_rgn, w_ret_out, g_qa, w_uq, g_qn, g_kva, w_uk, w_uv, g_kn, w_mla_out, w_out, g_ffn, w_gate, w_up, w_down, g_ple, w_ple_gate, w_ple_proj):
    B, T, D = x_prompt.shape
    Bs, Ts, _ = x_sample.shape
    assert Ts == 1, "the sample group decodes one token per request"
    depth = w_in.shape[0]
    conv_dim, width = w_dw.shape[2], w_dw.shape[1]
    q_lora, kv_lora = g_qa.shape[1], g_kva.shape[1]
    page = cache_ckv.shape[2]
    past_len = page_table.shape[1] * page
    assert T % RET_CHUNK == 0 and width - 1 <= HALO_ROWS <= T

    sizes = (2 * conv_dim, RET_HEADS * RET_DK, RET_HEADS * RET_DK, RET_HEADS * RET_DV, RET_HEADS * RET_DV,
             q_lora, kv_lora, MLA_ROPE, 3 * D)
    src = np.concatenate([[0], np.cumsum(sizes)]).tolist()
    kr_slab = jnp.pad(w_in[..., src[7]:src[8]], [(0, 0), (0, 0), (MLA_NOPE, LANE - MLA_QK)])
    w_in_p = jnp.concatenate([w_in[..., :src[6]], kr_slab, w_in[..., src[6]:src[7]], w_in[..., src[8]:]],
                             axis=-1).astype(bf16)
    offs = tuple(src[:7]) + (src[6] + LANE, src[7] + LANE, src[7] + LANE + 3 * D)

    row = lambda g: g[:, None, :]
    lane_pad = lambda g: jnp.pad(g, [(0, 0), (0, LANE - g.shape[-1])])[:, None, :]
    W = dict(
        g_mix=row(g_mix), w_in_p=w_in_p, g_qa=row(g_qa), g_kva=row(g_kva),
        w_uq_p=_pad_heads(w_uq, MLA_HEADS, MLA_QK).astype(bf16),
        g_qn_p=lane_pad(g_qn * (MLA_QK ** -0.5 * LOG2_E)),
        w_uk_p=_pad_heads(w_uk, MLA_HEADS, MLA_NOPE).astype(bf16),
        g_kn_p=lane_pad(g_kn),
        w_uv_b=w_uv.astype(bf16),
        w_ukt_b=jnp.swapaxes(w_uk, 1, 2).astype(bf16),
        w_ukt_p=jnp.swapaxes(_pad_heads(w_uk, MLA_HEADS, MLA_NOPE), 1, 2).astype(bf16),
        w_dw=w_dw, b_dw=row(b_dw), g_cln=row(g_cln), b_cln=row(b_cln), g_rgn=row(g_rgn),
        w_conv_out_b=w_conv_out.astype(bf16), w_ret_out_b=w_ret_out.astype(bf16),
        w_mla_out_b=w_mla_out.astype(bf16), w_out_b=w_out.astype(bf16), g_ffn=row(g_ffn),
        w_gate_b=w_gate.astype(bf16), w_up_b=w_up.astype(bf16), w_down_b=w_down.astype(bf16),
        g_ple=row(g_ple), w_ple_gate_b=w_ple_gate.astype(bf16), w_ple_proj_b=w_ple_proj.astype(bf16),
    )

    tabs_p = _rope_tables(jnp.arange(T, dtype=jnp.int32))
    tabs_s = _rope_tables(jnp.full((Bs,), past_len, jnp.int32))
    ret_tabs = _retention_tables(RET_CHUNK)
    g1 = _retention_tables(1)[0]

    tm = min(ROW_TILE, T)
    tq = min(ATT_TILE, T)
    tc = min(CONV_TILE, T)
    n_pages = page_table.shape[1]
    pp = next(c for c in (PAGES_PER_STEP, 8, 4, 2, 1) if n_pages % c == 0)

    cache_kpe_t = jnp.swapaxes(cache_kpe, 2, 3)

    xp = x_prompt.reshape(B * T, D)
    xs = x_sample.reshape(Bs, D)
    pp3 = p_prompt.reshape(depth, B * T, -1)
    ps3 = p_sample.reshape(depth, Bs, -1)
    outs = [[] for _ in range(8)]
    for l in range(depth):
        u, q, k, v, srg, qm, km, vm, ckv, kpe, gates = _in_proj(xp, tabs_p, l, W, offs, tm, T // tm)
        s3 = lambda a: a.reshape(B, T, a.shape[-1])
        u3 = s3(u)
        ca = _conv_prompt(u3, l, W, tc).reshape(B * T, -1)
        ra, s_new = _ret_prompt(s3(q), s3(k), s3(v), s3(srg), l, W, ret_tabs)
        om = _att_prompt(s3(qm), s3(km), s3(vm), tq)
        xp = _out_stage(xp, ca, ra.reshape(B * T, -1), om.reshape(B * T, -1), gates, pp3, l, W, tm)
        outs[0].append(s3(ckv)); outs[1].append(s3(kpe)); outs[4].append(s_new)
        outs[6].append(u3[:, T - (width - 1):])

        u, q, k, v, srg, qm, km, vm, ckv, kpe, gates = _in_proj(xs, tabs_s, l, W, offs, Bs, 1)
        ca = _conv_sample(state_conv, u, l, W)
        ra, s_new = _ret_sample(q, k, v, srg, state_ret, l, W, g1)
        om = _att_sample(qm, km, ckv, cache_ckv, cache_kpe_t, page_table, l, W, pp)
        xs = _out_stage(xs, ca, ra, om, gates, ps3, l, W, Bs)
        outs[2].append(ckv[:, None]); outs[3].append(kpe[:, None]); outs[5].append(s_new)
        outs[7].append(jnp.concatenate([state_conv[l][:, 1:], u[:, None]], axis=1))

    st = [jnp.stack(o) for o in outs]
    return (xp.reshape(B, T, D), xs.reshape(Bs, Ts, D), st[0], st[1], st[2], st[3], st[4], st[5], st[6], st[7])
```

```python
import functools

import jax
import jax.numpy as jnp
import numpy as np
from jax import lax
from jax.experimental import pallas as pl
from jax.experimental.pallas import tpu as pltpu

f32 = jnp.float32
bf16 = jnp.bfloat16

EPS = 1e-6
ROPE_BASE = 10000.0
RET_HEADS = 4
RET_DK = 128
RET_DV = 256
RET_CHUNK = 128
MLA_HEADS = 8
MLA_NOPE = 64
MLA_ROPE = 32
MLA_V = 64
MLA_QK = MLA_NOPE + MLA_ROPE
LANE = 128
SUBLANES = 8
HALO_ROWS = 32
VMEM_LIMIT = 56 * 1024 * 1024
MASK_VALUE = -1e30
LOG2_E = 1.4426950408889634
ROW_TILE = 256
ATT_TILE = 512
CONV_TILE = 256
RET_TILE = 512
PAGES_PER_STEP = 32

_NT = (((1,), (1,)), ((), ()))
_TN = (((0,), (0,)), ((), ()))


def _rms(x, g):
    return x * lax.rsqrt(jnp.mean(x * x, axis=-1, keepdims=True) + EPS) * g


def _const_spec(shape, index):
    return pl.BlockSpec(shape, index, pipeline_mode=pl.Buffered(1))


def _params(sem):
    return pltpu.CompilerParams(dimension_semantics=sem, vmem_limit_bytes=VMEM_LIMIT)


def _mla_rope(x, cm, s1, s2):
    return x * cm + pltpu.roll(x, MLA_ROPE // 2, 1) * s1 + pltpu.roll(x, LANE - MLA_ROPE // 2, 1) * s2


def _in_proj_kernel(x_ref, c2r_ref, s2r_ref, cm_ref, s1m_ref, s2m_ref, gmix_ref, win_ref, gqa_ref, wuq_ref,
                    gqn_ref, gkva_ref, wuk_ref, wuv_ref, gkn_ref,
                    u_ref, q_ref, k_ref, v_ref, srg_ref, qm_ref, km_ref, vm_ref, ckv_ref, kpe_ref, gates_ref,
                    *, offs):
    o_glu, o_rq, o_rk, o_rv, o_rg, o_qa, o_kr, o_kva, o_gates, o_end = offs
    h = _rms(x_ref[...], gmix_ref[...]).astype(bf16)

    def proj(c0, c1):
        return jnp.dot(h, win_ref[:, c0:c1], preferred_element_type=f32)

    glu = proj(o_glu, o_rq)
    half = (o_rq - o_glu) // 2
    u_ref[...] = glu[:, :half] * jax.nn.sigmoid(glu[:, half:])

    c2, s2 = c2r_ref[...], s2r_ref[...]
    rq = proj(o_rq, o_rk)
    rk = proj(o_rk, o_rv)
    for hh in range(RET_HEADS):
        sl = slice(hh * RET_DK, (hh + 1) * RET_DK)
        xq, xk = rq[:, sl], rk[:, sl]
        q_ref[:, sl] = (xq * c2 + pltpu.roll(xq, RET_DK // 2, 1) * s2).astype(bf16)
        k_ref[:, sl] = ((xk * c2 + pltpu.roll(xk, RET_DK // 2, 1) * s2) * (RET_DK ** -0.5)).astype(bf16)
    v_ref[...] = proj(o_rv, o_rg).astype(bf16)
    srg_ref[...] = jax.nn.silu(proj(o_rg, o_qa)).astype(bf16)

    cm, s1, s2m = cm_ref[...], s1m_ref[...], s2m_ref[...]
    qa_kr = proj(o_qa, o_kva)
    c_q = _rms(qa_kr[:, :o_kr - o_qa], gqa_ref[...]).astype(bf16)
    qm = jnp.dot(c_q, wuq_ref[...], preferred_element_type=f32)
    gqn = gqn_ref[...]
    for hh in range(MLA_HEADS):
        sl = slice(hh * LANE, (hh + 1) * LANE)
        r = _mla_rope(qm[:, sl], cm, s1, s2m)
        ss = jnp.sum(r * r, axis=-1, keepdims=True)
        qm_ref[:, sl] = (r * lax.rsqrt(ss / MLA_QK + EPS) * gqn).astype(bf16)

    ckv = _rms(proj(o_kva, o_gates), gkva_ref[...])
    ckv_ref[...] = ckv
    cb = ckv.astype(bf16)
    kpe_slab = _mla_rope(qa_kr[:, o_kr - o_qa:], cm, s1, s2m)
    kpe_ref[...] = kpe_slab[:, MLA_NOPE:MLA_QK]
    kn = jnp.dot(cb, wuk_ref[...], preferred_element_type=f32)
    gkn = gkn_ref[...]
    for hh in range(MLA_HEADS):
        sl = slice(hh * LANE, (hh + 1) * LANE)
        kk = kn[:, sl] + kpe_slab
        ss = jnp.sum(kk * kk, axis=-1, keepdims=True)
        km_ref[:, sl] = (kk * lax.rsqrt(ss / MLA_QK + EPS) * gkn).astype(bf16)
    vm_ref[...] = jnp.dot(cb, wuv_ref[...], preferred_element_type=f32).astype(bf16)

    gates_ref[...] = jax.nn.sigmoid(proj(o_gates, o_end)).astype(bf16)


def _in_proj(x2d, tabs, l, W, offs, tm, n_pos_blocks):
    M, D = x2d.shape
    conv_dim = (offs[1] - offs[0]) // 2
    n_gates = offs[9] - offs[8]
    kv_lora = offs[8] - offs[7]
    rows = lambda n: pl.BlockSpec((tm, n), lambda i: (i, 0))
    tab = pl.BlockSpec((tm, LANE), lambda i: (i % n_pos_blocks, 0))
    wl = lambda a: _const_spec((None,) + a.shape[1:], lambda i: (l,) + (0,) * (a.ndim - 1))
    out_dims = [(conv_dim, f32), (RET_HEADS * RET_DK, bf16), (RET_HEADS * RET_DK, bf16), (RET_HEADS * RET_DV, bf16),
                (RET_HEADS * RET_DV, bf16), (MLA_HEADS * LANE, bf16), (MLA_HEADS * LANE, bf16),
                (MLA_HEADS * MLA_V, bf16), (kv_lora, f32), (MLA_ROPE, f32), (n_gates, bf16)]
    weights = [W['g_mix'], W['w_in_p'], W['g_qa'], W['w_uq_p'], W['g_qn_p'], W['g_kva'], W['w_uk_p'], W['w_uv_b'],
               W['g_kn_p']]
    return pl.pallas_call(
        functools.partial(_in_proj_kernel, offs=offs),
        grid=(M // tm,),
        in_specs=[rows(D)] + [tab] * 5 + [wl(a) for a in weights],
        out_specs=[rows(n) for n, _ in out_dims],
        out_shape=[jax.ShapeDtypeStruct((M, n), dt) for n, dt in out_dims],
        compiler_params=_params(("parallel",)),
        name="in_proj",
    )(x2d, *tabs, *weights)


def _ln_swish(y, g, b):
    mu = jnp.mean(y, axis=-1, keepdims=True)
    d = y - mu
    var = jnp.mean(d * d, axis=-1, keepdims=True)
    z = d * lax.rsqrt(var + EPS) * g + b
    return z * jax.nn.sigmoid(z)


def _conv_prompt_kernel(prev_ref, cur_ref, w_ref, b_ref, g_ref, bl_ref, o_ref, ubuf, sbuf, *, tc, width, rc):
    t = pl.program_id(1)
    ubuf[0:HALO_ROWS, :] = jnp.where(t > 0, prev_ref[0], 0.0)
    ubuf[HALO_ROWS:, :] = cur_ref[0]
    first = HALO_ROWS - (width - 1)
    C = cur_ref.shape[-1]
    R = HALO_ROWS + tc
    sbuf[0] = ubuf[...]
    for ph in range(1, SUBLANES):
        sbuf[ph, 0:R - SUBLANES, :] = ubuf[ph:R - SUBLANES + ph, :]
    for r0 in range(0, tc, rc):
        cols = []
        for c0 in range(0, C, LANE):
            acc = jnp.zeros((rc, LANE), f32) + b_ref[:, c0:c0 + LANE]
            for j in range(width):
                ph = (first + j) % SUBLANES
                base = first + j - ph + r0
                acc = acc + w_ref[j:j + 1, c0:c0 + LANE] * sbuf[ph, base:base + rc, c0:c0 + LANE]
            cols.append(acc)
        y = jnp.concatenate(cols, axis=-1)
        o_ref[0, r0:r0 + rc, :] = _ln_swish(y, g_ref[...], bl_ref[...]).astype(bf16)


def _conv_prompt(u3, l, W, tc):
    B, T, C = u3.shape
    width = W['w_dw'].shape[1]
    hb = tc // HALO_ROWS
    wl = lambda a: _const_spec((None,) + a.shape[1:], lambda b, t: (l,) + (0,) * (a.ndim - 1))
    weights = [W['w_dw'], W['b_dw'], W['g_cln'], W['b_cln']]
    return pl.pallas_call(
        functools.partial(_conv_prompt_kernel, tc=tc, width=width, rc=min(64, tc)),
        grid=(B, T // tc),
        in_specs=[pl.BlockSpec((1, HALO_ROWS, C), lambda b, t: (b, jnp.maximum(t * hb - 1, 0), 0)),
                  pl.BlockSpec((1, tc, C), lambda b, t: (b, t, 0))] + [wl(a) for a in weights],
        out_specs=pl.BlockSpec((1, tc, C), lambda b, t: (b, t, 0)),
        out_shape=jax.ShapeDtypeStruct((B, T, C), bf16),
        scratch_shapes=[pltpu.VMEM((HALO_ROWS + tc, C), f32), pltpu.VMEM((SUBLANES, HALO_ROWS + tc, C), f32)],
        compiler_params=_params(("parallel", "parallel")),
        name="conv_prompt",
    )(u3, u3, *weights)


def _conv_sample_kernel(st_ref, u_ref, w_ref, b_ref, g_ref, bl_ref, o_ref, *, width):
    y = u_ref[...] * w_ref[width - 1:width, :] + b_ref[...]
    for j in range(width - 1):
        y = y + st_ref[:, j, :] * w_ref[j:j + 1, :]
    o_ref[...] = _ln_swish(y, g_ref[...], bl_ref[...]).astype(bf16)


def _conv_sample(state, u2, l, W):
    Bs, C = u2.shape
    width = W['w_dw'].shape[1]
    wl = lambda a: pl.BlockSpec((None,) + a.shape[1:], lambda i: (l,) + (0,) * (a.ndim - 1))
    weights = [W['w_dw'], W['b_dw'], W['g_cln'], W['b_cln']]
    return pl.pallas_call(
        functools.partial(_conv_sample_kernel, width=width),
        grid=(1,),
        in_specs=[pl.BlockSpec((None,) + state.shape[1:], lambda i: (l, 0, 0, 0)),
                  pl.BlockSpec(u2.shape, lambda i: (0, 0))] + [wl(a) for a in weights],
        out_specs=pl.BlockSpec((Bs, C), lambda i: (0, 0)),
        out_shape=jax.ShapeDtypeStruct((Bs, C), bf16),
        compiler_params=_params(("arbitrary",)),
        name="conv_sample",
    )(state, u2, *weights)


def _group_norm_gate(o, g, gate):
    mu = jnp.mean(o, axis=-1, keepdims=True)
    d = o - mu
    var = jnp.mean(d * d, axis=-1, keepdims=True)
    return (d * lax.rsqrt(var + EPS) * g * gate).astype(bf16)


def _ret_prompt_kernel(gc_ref, q_ref, k_ref, v_ref, srg_ref, decay_ref, cross_ref, kdec_ref, g_ref,
                       o_ref, s_out_ref, s_ref):
    c = pl.program_id(1)

    @pl.when(c == 0)
    def _():
        s_ref[...] = jnp.zeros_like(s_ref)

    C = RET_CHUNK
    for hh in range(RET_HEADS):
        ks = slice(hh * RET_DK, (hh + 1) * RET_DK)
        vs = slice(hh * RET_DV, (hh + 1) * RET_DV)
        cr = jnp.concatenate([cross_ref[hh]] * (RET_DV // LANE), axis=-1)
        s_cur = s_ref[hh]
        for r0 in range(0, q_ref.shape[1], C):
            rows = slice(r0, r0 + C)
            q, k, v = q_ref[0, rows, ks], k_ref[0, rows, ks], v_ref[0, rows, vs]
            scores = lax.dot_general(q, k, _NT, preferred_element_type=f32) * decay_ref[hh]
            o = jnp.dot(scores.astype(bf16), v, preferred_element_type=f32)
            o = o + jnp.dot(q, s_cur.astype(bf16), preferred_element_type=f32) * cr
            kd = (k.astype(f32) * kdec_ref[hh]).astype(bf16)
            s_cur = gc_ref[hh] * s_cur + lax.dot_general(kd, v, _TN, preferred_element_type=f32)
            o_ref[0, rows, vs] = _group_norm_gate(o, g_ref[:, vs], srg_ref[0, rows, vs].astype(f32))
        s_ref[hh] = s_cur

    @pl.when(c == pl.num_programs(1) - 1)
    def _():
        s_out_ref[0] = s_ref[...]


def _ret_prompt(q3, k3, v3, srg3, l, W, tabs, tr):
    B, T, _ = q3.shape
    gc, decay, cross, kdec = tabs
    blk = lambda n: pl.BlockSpec((1, tr, n), lambda b, c: (b, c, 0))
    tab = _const_spec((RET_HEADS, RET_CHUNK, LANE), lambda b, c: (0, 0, 0))
    return pl.pallas_call(
        _ret_prompt_kernel,
        grid=(B, T // tr),
        in_specs=[pl.BlockSpec(memory_space=pltpu.SMEM), blk(RET_HEADS * RET_DK), blk(RET_HEADS * RET_DK),
                  blk(RET_HEADS * RET_DV), blk(RET_HEADS * RET_DV), tab, tab, tab,
                  _const_spec((None, 1, RET_HEADS * RET_DV), lambda b, c: (l, 0, 0))],
        out_specs=[blk(RET_HEADS * RET_DV),
                   pl.BlockSpec((1, RET_HEADS, RET_DK, RET_DV), lambda b, c: (b, 0, 0, 0))],
        out_shape=[jax.ShapeDtypeStruct((B, T, RET_HEADS * RET_DV), bf16),
                   jax.ShapeDtypeStruct((B, RET_HEADS, RET_DK, RET_DV), f32)],
        scratch_shapes=[pltpu.VMEM((RET_HEADS, RET_DK, RET_DV), f32)],
        compiler_params=_params(("parallel", "arbitrary")),
        name="ret_prompt",
    )(gc, q3, k3, v3, srg3, decay, cross, kdec, W['g_rgn'])


def _row_to_col(row, n):
    eye = lax.broadcasted_iota(jnp.int32, (n, n), 0) == lax.broadcasted_iota(jnp.int32, (n, n), 1)
    return jnp.sum(jnp.where(eye, jnp.broadcast_to(row, (n, n)), 0.0), axis=-1, keepdims=True)


def _ret_sample_kernel(g1_ref, q_ref, k_ref, v_ref, srg_ref, s_ref, g_ref, o_ref, s_out_ref):
    for i in range(q_ref.shape[0]):
        for hh in range(RET_HEADS):
            ks = slice(hh * RET_DK, (hh + 1) * RET_DK)
            vs = slice(hh * RET_DV, (hh + 1) * RET_DV)
            qcol = _row_to_col(q_ref[i, :, ks].astype(f32), RET_DK)
            kcol = _row_to_col(k_ref[i, :, ks].astype(f32), RET_DK)
            v = v_ref[i, :, vs].astype(f32)
            s_new = g1_ref[hh] * s_ref[i, hh] + kcol * v
            s_out_ref[i, hh] = s_new
            o = jnp.sum(qcol * s_new, axis=0, keepdims=True)
            o_ref[i, :, vs] = _group_norm_gate(o, g_ref[:, vs], srg_ref[i, :, vs].astype(f32))


def _ret_sample(q2, k2, v2, srg2, state, l, W, g1):
    Bs = q2.shape[0]
    nb = next(c for c in (4, 2, 1) if Bs % c == 0)
    r3 = lambda a: a.reshape(Bs, 1, a.shape[-1])
    row = lambda n: pl.BlockSpec((nb, 1, n), lambda b: (b, 0, 0))
    sblk = (nb, RET_HEADS, RET_DK, RET_DV)
    o, s_new = pl.pallas_call(
        _ret_sample_kernel,
        grid=(Bs // nb,),
        in_specs=[pl.BlockSpec(memory_space=pltpu.SMEM), row(RET_HEADS * RET_DK), row(RET_HEADS * RET_DK),
                  row(RET_HEADS * RET_DV), row(RET_HEADS * RET_DV),
                  pl.BlockSpec((None,) + sblk, lambda b: (l, b, 0, 0, 0)),
                  pl.BlockSpec((None, 1, RET_HEADS * RET_DV), lambda b: (l, 0, 0))],
        out_specs=[row(RET_HEADS * RET_DV), pl.BlockSpec(sblk, lambda b: (b, 0, 0, 0))],
        out_shape=[jax.ShapeDtypeStruct((Bs, 1, RET_HEADS * RET_DV), bf16),
                   jax.ShapeDtypeStruct((Bs, RET_HEADS, RET_DK, RET_DV), f32)],
        compiler_params=_params(("parallel",)),
        name="ret_sample",
    )(g1, r3(q2), r3(k2), r3(v2), r3(srg2), state, W['g_rgn'])
    return o.reshape(Bs, -1), s_new


def _att_prompt_kernel(qi_ref, ki_ref, q_ref, k_ref, v_ref, o_ref, m_ref, l_ref, acc_ref):
    s_idx = pl.program_id(1)
    qi, ki = qi_ref[s_idx], ki_ref[s_idx]
    tq, tk = q_ref.shape[1], k_ref.shape[1]
    n_lt = tk // LANE

    @pl.when(ki == 0)
    def _():
        m_ref[...] = jnp.full_like(m_ref, MASK_VALUE)
        l_ref[...] = jnp.zeros_like(l_ref)
        acc_ref[...] = jnp.zeros_like(acc_ref)

    def step(diagonal):
        for hd in range(MLA_HEADS):
            v = v_ref[0, :, (hd // 2) * LANE:(hd // 2 + 1) * LANE]
            sl = slice(hd * LANE, (hd + 1) * LANE)
            s = lax.dot_general(q_ref[0, :, sl], k_ref[0, :, sl], _NT, preferred_element_type=f32)
            if diagonal:
                row = lax.broadcasted_iota(jnp.int32, (tq, tk), 0)
                col = lax.broadcasted_iota(jnp.int32, (tq, tk), 1)
                s = jnp.where(col <= row, s, MASK_VALUE)
            m_prev = m_ref[hd]
            smax = s[:, 0:LANE]
            for c in range(1, n_lt):
                smax = jnp.maximum(smax, s[:, c * LANE:(c + 1) * LANE])
            m_new = jnp.maximum(m_prev, jnp.max(smax, axis=-1, keepdims=True))
            a = jnp.exp2(m_prev - m_new)
            p = jnp.exp2(s - jnp.concatenate([m_new] * n_lt, axis=-1))
            psum = p[:, 0:LANE]
            for c in range(1, n_lt):
                psum = psum + p[:, c * LANE:(c + 1) * LANE]
            l_ref[hd] = a * l_ref[hd] + psum
            acc_ref[hd] = a * acc_ref[hd] + jnp.dot(p.astype(bf16), v, preferred_element_type=f32)
            m_ref[hd] = m_new

    @pl.when(ki < qi)
    def _():
        step(False)

    @pl.when(ki == qi)
    def _():
        step(True)
        lane = lax.broadcasted_iota(jnp.int32, (tq, LANE), 1)
        for g in range(MLA_HEADS // 2):
            l0 = jnp.sum(l_ref[2 * g], axis=-1, keepdims=True)
            l1 = jnp.sum(l_ref[2 * g + 1], axis=-1, keepdims=True)
            o = jnp.where(lane < MLA_V, acc_ref[2 * g] / l0, acc_ref[2 * g + 1] / l1)
            o_ref[0, :, g * LANE:(g + 1) * LANE] = o.astype(bf16)


def _att_prompt(qm3, km3, vm3, tq):
    B, T, _ = qm3.shape
    nq = T // tq
    H = MLA_HEADS
    pairs = [(qi, ki) for qi in range(nq) for ki in range(qi + 1)]
    qi_tab = jnp.asarray([p[0] for p in pairs], jnp.int32)
    ki_tab = jnp.asarray([p[1] for p in pairs], jnp.int32)
    grid_spec = pltpu.PrefetchScalarGridSpec(
        num_scalar_prefetch=2,
        grid=(B, len(pairs)),
        in_specs=[pl.BlockSpec((1, tq, H * LANE), lambda b, s, qi, ki: (b, qi[s], 0)),
                  pl.BlockSpec((1, tq, H * LANE), lambda b, s, qi, ki: (b, ki[s], 0)),
                  pl.BlockSpec((1, tq, H * MLA_V), lambda b, s, qi, ki: (b, ki[s], 0))],
        out_specs=pl.BlockSpec((1, tq, H * MLA_V), lambda b, s, qi, ki: (b, qi[s], 0)),
        scratch_shapes=[pltpu.VMEM((H, tq, LANE), f32), pltpu.VMEM((H, tq, LANE), f32),
                        pltpu.VMEM((H, tq, LANE), f32)],
    )
    return pl.pallas_call(
        _att_prompt_kernel,
        grid_spec=grid_spec,
        out_shape=jax.ShapeDtypeStruct((B, T, H * MLA_V), bf16),
        compiler_params=_params(("parallel", "arbitrary")),
        name="att_prompt",
    )(qi_tab, ki_tab, qm3, km3, vm3)


def _hi_lo(x):
    hi = x.astype(bf16).astype(f32)
    return jnp.concatenate([hi, x - hi], axis=0).astype(bf16)


def _rows_select(rows, n=None):
    n, w = n or len(rows), rows[0].shape[-1]
    sub = lax.broadcasted_iota(jnp.int32, (n, w), 0)
    out = jnp.zeros((n, w), rows[0].dtype)
    for i, r in enumerate(rows):
        out = jnp.where(sub == i, jnp.broadcast_to(r, (n, w)), out)
    return out


def _att_sample_kernel(pt_ref, ckv_hbm, kpe_hbm, qm_ref, kmn_ref, ckvn_ref, gkn_ref, wukt_ref, wuktp_ref, wuv_ref,
                       o_ref, ckv_buf, kpe_buf, sem_c, sem_k, aq_ref, cbb_ref, *, l, pp, nj):
    b = pl.program_id(0)
    nb = pl.num_programs(0)
    H = MLA_HEADS
    page = ckv_hbm.shape[2]

    def copies(bb, j, slot):
        out = []
        for i in range(pp):
            pg = pt_ref[bb, j * pp + i]
            out.append(pltpu.make_async_copy(ckv_hbm.at[l, pg], ckv_buf.at[slot, pl.ds(i * page, page)],
                                             sem_c.at[slot]))
            out.append(pltpu.make_async_copy(kpe_hbm.at[l, pg], kpe_buf.at[slot, :, pl.ds(i * page, page)],
                                             sem_k.at[slot]))
        return out

    @pl.when(b == 0)
    def _():
        for c in copies(0, 0, 0) + copies(0, 1, 1):
            c.start()

    qg = qm_ref[0].astype(f32) * jnp.concatenate([gkn_ref[...]] * H, axis=-1)
    sub = lax.broadcasted_iota(jnp.int32, (2 * H, H * LANE), 0)
    head = lax.broadcasted_iota(jnp.int32, (2 * H, H * LANE), 1) // LANE
    qbd = jnp.where(sub == head, jnp.broadcast_to(qg, (2 * H, H * LANE)), 0.0).astype(bf16)
    qabs = jnp.dot(qbd, wuktp_ref[...], preferred_element_type=f32)[0:H]
    aq_ref[0:H * MLA_NOPE, :] = wukt_ref[...]
    aq_ref[H * MLA_NOPE:, :] = _hi_lo(qabs)
    qr = _rows_select([qg[:, h * LANE + MLA_NOPE:h * LANE + MLA_QK] for h in range(H)], 2 * H).astype(bf16)

    def scores(j):
        slot = j % 2
        for c in copies(b, j, slot):
            c.wait()
        cbb_ref[slot] = ckv_buf[slot].astype(bf16)
        cb = cbb_ref[slot]
        kt = kpe_buf[slot]
        r = lax.dot_general(aq_ref[...], cb, _NT, preferred_element_type=f32)
        ss = jnp.concatenate(
            [jnp.sum(jnp.square(r[h * MLA_NOPE:(h + 1) * MLA_NOPE]), axis=0, keepdims=True) for h in range(H)],
            axis=0)
        ss_rope = jnp.sum(kt * kt, axis=0, keepdims=True)
        s_rope = jnp.dot(qr, kt.astype(bf16), preferred_element_type=f32)[0:H]
        s_nope = r[H * MLA_NOPE:H * MLA_NOPE + H] + r[H * MLA_NOPE + H:]
        s = (s_nope + s_rope) * lax.rsqrt((ss + ss_rope) / MLA_QK + EPS)
        if j + 2 < nj:
            for c in copies(b, j + 2, slot):
                c.start()
        else:
            @pl.when(b + 1 < nb)
            def _():
                for c in copies(b + 1, j + 2 - nj, slot):
                    c.start()
        return s

    def accumulate(j, s, m, lsum, ctx):
        m_new = jnp.maximum(m, jnp.max(s, axis=-1, keepdims=True))
        a = jnp.exp2(m - m_new)
        p = jnp.exp2(s - m_new)
        lsum = a * lsum + jnp.sum(p, axis=-1, keepdims=True)
        p16 = jnp.concatenate([p, jnp.zeros_like(p)], axis=0).astype(bf16)
        ctx = a * ctx + jnp.dot(p16, cbb_ref[j % 2], preferred_element_type=f32)[0:H]
        return m_new, lsum, ctx

    m = jnp.full((H, 1), MASK_VALUE, f32)
    lsum = jnp.zeros((H, 1), f32)
    ctx = jnp.zeros((H, ckvn_ref.shape[-1]), f32)
    s_prev = scores(0)
    for j in range(1, nj):
        s_cur = scores(j)
        m, lsum, ctx = accumulate(j - 1, s_prev, m, lsum, ctx)
        s_prev = s_cur
    m, lsum, ctx = accumulate(nj - 1, s_prev, m, lsum, ctx)

    prod = qm_ref[0].astype(f32) * kmn_ref[0].astype(f32)
    s_new = jnp.sum(_rows_select([prod[:, h * LANE:(h + 1) * LANE] for h in range(H)]), axis=-1, keepdims=True)
    m_fin = jnp.maximum(m, s_new)
    a = jnp.exp2(m - m_fin)
    pn = jnp.exp2(s_new - m_fin)
    l_fin = a * lsum + pn
    c_new = ckvn_ref[0].astype(bf16).astype(f32)
    ctx = (a * ctx + pn.astype(bf16).astype(f32) * c_new) / l_fin
    both = jnp.dot(_hi_lo(ctx), wuv_ref[...], preferred_element_type=f32)
    full = both[0:H] + both[H:]
    sub = lax.broadcasted_iota(jnp.int32, full.shape, 0)
    head = lax.broadcasted_iota(jnp.int32, full.shape, 1) // MLA_V
    o_ref[0] = jnp.sum(jnp.where(sub == head, full, 0.0), axis=0, keepdims=True).astype(bf16)


def _att_sample(qm2, kmn2, ckvn2, cache_ckv, cache_kpe_t, page_table, l, W, pp):
    Bs = qm2.shape[0]
    n_pages = page_table.shape[1]
    page, n_lat = cache_ckv.shape[2], cache_ckv.shape[3]
    rope = cache_kpe_t.shape[2]
    H = MLA_HEADS
    nj = n_pages // pp
    assert nj % 2 == 0, "the two staging slots alternate by step parity across requests"
    r3 = lambda a: a.reshape(Bs, 1, a.shape[-1])
    row = lambda n: pl.BlockSpec((1, 1, n), lambda b, pt: (b, 0, 0))
    wl = lambda a: _const_spec((None,) + a.shape[1:], lambda b, pt: (l,) + (0,) * (a.ndim - 1))
    weights = [W['g_kn_p'], W['w_ukt_b'], W['w_ukt_p'], W['w_uv_b']]
    grid_spec = pltpu.PrefetchScalarGridSpec(
        num_scalar_prefetch=1,
        grid=(Bs,),
        in_specs=([pl.BlockSpec(memory_space=pl.ANY), pl.BlockSpec(memory_space=pl.ANY)]
                  + [row(H * LANE), row(H * LANE), row(n_lat)] + [wl(a) for a in weights]),
        out_specs=row(H * MLA_V),
        scratch_shapes=[pltpu.VMEM((2, pp * page, n_lat), f32), pltpu.VMEM((2, rope, pp * page), f32),
                        pltpu.SemaphoreType.DMA((2,)), pltpu.SemaphoreType.DMA((2,)),
                        pltpu.VMEM((H * MLA_NOPE + 2 * H, n_lat), bf16), pltpu.VMEM((2, pp * page, n_lat), bf16)],
    )
    o = pl.pallas_call(
        functools.partial(_att_sample_kernel, l=l, pp=pp, nj=nj),
        grid_spec=grid_spec,
        out_shape=jax.ShapeDtypeStruct((Bs, 1, H * MLA_V), bf16),
        compiler_params=_params(("arbitrary",)),
        name="att_sample",
    )(page_table, cache_ckv, cache_kpe_t, r3(qm2), r3(kmn2), r3(ckvn2), *weights)
    return o.reshape(Bs, -1)


def _out_kernel(x_ref, ca_ref, ra_ref, om_ref, gates_ref, p_ref, wco_ref, wro_ref, wmo_ref, wout_ref, gffn_ref,
                wg_ref, wu_ref, wd_ref, gple_ref, wpg_ref, wpp_ref, o_ref):
    D = x_ref.shape[-1]
    dot = lambda a, b: jnp.dot(a, b, preferred_element_type=f32)
    gate = lambda i: gates_ref[:, i * D:(i + 1) * D].astype(f32)
    mix = (gate(0) * dot(ca_ref[...], wco_ref[...]) + gate(1) * dot(ra_ref[...], wro_ref[...])
           + gate(2) * dot(om_ref[...], wmo_ref[...]))
    x = x_ref[...] + dot(mix.astype(bf16), wout_ref[...])
    hf = _rms(x, gffn_ref[...]).astype(bf16)
    ff = jax.nn.silu(dot(hf, wg_ref[...])) * dot(hf, wu_ref[...])
    x = x + dot(ff.astype(bf16), wd_ref[...])
    hp = _rms(x, gple_ref[...]).astype(bf16)
    pg = jax.nn.sigmoid(dot(hp, wpg_ref[...]))
    o_ref[...] = x + pg * dot(p_ref[...].astype(bf16), wpp_ref[...])


def _out_stage(x2d, ca, ra, om, gates, p3, l, W, tm):
    M, D = x2d.shape
    rows = lambda n: pl.BlockSpec((tm, n), lambda i: (i, 0))
    wl = lambda a: _const_spec((None,) + a.shape[1:], lambda i: (l,) + (0,) * (a.ndim - 1))
    weights = [W['w_conv_out_b'], W['w_ret_out_b'], W['w_mla_out_b'], W['w_out_b'], W['g_ffn'], W['w_gate_b'],
               W['w_up_b'], W['w_down_b'], W['g_ple'], W['w_ple_gate_b'], W['w_ple_proj_b']]
    return pl.pallas_call(
        _out_kernel,
        grid=(M // tm,),
        in_specs=[rows(D), rows(ca.shape[1]), rows(ra.shape[1]), rows(om.shape[1]), rows(gates.shape[1]),
                  pl.BlockSpec((None, tm, p3.shape[-1]), lambda i: (l, i, 0))] + [wl(a) for a in weights],
        out_specs=rows(D),
        out_shape=jax.ShapeDtypeStruct((M, D), f32),
        compiler_params=_params(("parallel",)),
        name="out_stage",
    )(x2d, ca, ra, om, gates, p3, *weights)


def _rope_tables(pos):
    n = pos.shape[0]
    posf = pos.astype(f32)[:, None]

    def cs(half):
        inv = ROPE_BASE ** (-jnp.arange(half, dtype=f32) / half)
        ang = posf * inv[None, :]
        return jnp.cos(ang), jnp.sin(ang)

    c, s = cs(RET_DK // 2)
    c2r = jnp.concatenate([c, c], axis=-1)
    s2r = jnp.concatenate([-s, s], axis=-1)
    c, s = cs(MLA_ROPE // 2)
    hr = MLA_ROPE // 2
    z = lambda w: jnp.zeros((n, w), f32)
    cm = jnp.concatenate([jnp.ones((n, MLA_NOPE), f32), c, c, z(LANE - MLA_QK)], axis=-1)
    s1m = jnp.concatenate([z(MLA_NOPE + hr), s, z(LANE - MLA_QK)], axis=-1)
    s2m = jnp.concatenate([z(MLA_NOPE), -s, z(LANE - MLA_NOPE - hr)], axis=-1)
    return c2r, s2r, cm, s1m, s2m


def _retention_tables(C):
    log_gamma = jnp.log(1.0 - 2.0 ** (-5.0 - jnp.arange(RET_HEADS, dtype=f32)))
    idx = jnp.arange(C, dtype=f32)
    diff = idx[:, None] - idx[None, :]
    decay = jnp.where(diff >= 0, jnp.exp(jnp.maximum(diff, 0.0)[None] * log_gamma[:, None, None]), 0.0)
    cross = jnp.exp((idx + 1.0)[:, None] * log_gamma[None, :])
    kdec = jnp.exp((C - 1.0 - idx)[:, None] * log_gamma[None, :])
    gc = jnp.exp(C * log_gamma)
    lanes = lambda a: jnp.broadcast_to(a.T[:, :, None], (RET_HEADS, C, LANE))
    return gc, decay, lanes(cross), lanes(kdec)


def _pad_heads(w, heads, width):
    w = w.reshape(w.shape[:-1] + (heads, width))
    w = jnp.pad(w, [(0, 0)] * (w.ndim - 1) + [(0, LANE - width)])
    return w.reshape(w.shape[:-2] + (heads * LANE,))


def kernel(x_prompt, x_sample, p_prompt, p_sample, cache_ckv, cache_kpe, state_ret, state_conv, page_table, g_mix, w_in, w_dw, b_dw, g_cln, b_cln, w_conv_out, g_rgn, w_ret_out, g_qa, w_uq, g_qn, g_kva, w_uk, w_uv, g_kn, w_mla_out, w_out, g_ffn, w_gate, w_up, w_down, g_ple, w_ple_gate, w_ple_proj):
    B, T, D = x_prompt.shape
    Bs, Ts, _ = x_sample.shape
    assert Ts == 1, "the sample group decodes one token per request"
    depth = w_in.shape[0]
    conv_dim, width = w_dw.shape[2], w_dw.shape[1]
    q_lora, kv_lora = g_qa.shape[1], g_kva.shape[1]
    page = cache_ckv.shape[2]
    past_len = page_table.shape[1] * page
    assert T % RET_CHUNK == 0 and width - 1 <= HALO_ROWS <= T

    sizes = (2 * conv_dim, RET_HEADS * RET_DK, RET_HEADS * RET_DK, RET_HEADS * RET_DV, RET_HEADS * RET_DV,
             q_lora, kv_lora, MLA_ROPE, 3 * D)
    src = np.concatenate([[0], np.cumsum(sizes)]).tolist()
    kr_slab = jnp.pad(w_in[..., src[7]:src[8]], [(0, 0), (0, 0), (MLA_NOPE, LANE - MLA_QK)])
    w_in_p = jnp.concatenate([w_in[..., :src[6]], kr_slab, w_in[..., src[6]:src[7]], w_in[..., src[8]:]],
                             axis=-1).astype(bf16)
    offs = tuple(src[:7]) + (src[6] + LANE, src[7] + LANE, src[7] + LANE + 3 * D)

    row = lambda g: g[:, None, :]
    lane_pad = lambda g: jnp.pad(g, [(0, 0), (0, LANE - g.shape[-1])])[:, None, :]
    W = dict(
        g_mix=row(g_mix), w_in_p=w_in_p, g_qa=row(g_qa), g_kva=row(g_kva),
        w_uq_p=_pad_heads(w_uq, MLA_HEADS, MLA_QK).astype(bf16),
        g_qn_p=lane_pad(g_qn * (MLA_QK ** -0.5 * LOG2_E)),
        w_uk_p=_pad_heads(w_uk, MLA_HEADS, MLA_NOPE).astype(bf16),
        g_kn_p=lane_pad(g_kn),
        w_uv_b=w_uv.astype(bf16),
        w_ukt_b=jnp.swapaxes(w_uk, 1, 2).astype(bf16),
        w_ukt_p=jnp.swapaxes(_pad_heads(w_uk, MLA_HEADS, MLA_NOPE), 1, 2).astype(bf16),
        w_dw=w_dw, b_dw=row(b_dw), g_cln=row(g_cln), b_cln=row(b_cln), g_rgn=row(g_rgn),
        w_conv_out_b=w_conv_out.astype(bf16), w_ret_out_b=w_ret_out.astype(bf16),
        w_mla_out_b=w_mla_out.astype(bf16), w_out_b=w_out.astype(bf16), g_ffn=row(g_ffn),
        w_gate_b=w_gate.astype(bf16), w_up_b=w_up.astype(bf16), w_down_b=w_down.astype(bf16),
        g_ple=row(g_ple), w_ple_gate_b=w_ple_gate.astype(bf16), w_ple_proj_b=w_ple_proj.astype(bf16),
    )

    tabs_p = _rope_tables(jnp.arange(T, dtype=jnp.int32))
    tabs_s = _rope_tables(jnp.full((Bs,), past_len, jnp.int32))
    ret_tabs = _retention_tables(RET_CHUNK)
    g1 = _retention_tables(1)[0]

    tm = min(ROW_TILE, T)
    tq = min(ATT_TILE, T)
    tc = min(CONV_TILE, T)
    tr = min(RET_TILE, T)
    n_pages = page_table.shape[1]
    pp = next(c for c in (PAGES_PER_STEP, 8, 4, 2, 1) if n_pages % (2 * c) == 0)

    cache_kpe_t = jnp.swapaxes(cache_kpe, 2, 3)

    xp = x_prompt.reshape(B * T, D)
    xs = x_sample.reshape(Bs, D)
    pp3 = p_prompt.reshape(depth, B * T, -1)
    ps3 = p_sample.reshape(depth, Bs, -1)
    outs = [[] for _ in range(8)]
    for l in range(depth):
        u, q, k, v, srg, qm, km, vm, ckv, kpe, gates = _in_proj(xp, tabs_p, l, W, offs, tm, T // tm)
        s3 = lambda a: a.reshape(B, T, a.shape[-1])
        u3 = s3(u)
        ca = _conv_prompt(u3, l, W, tc).reshape(B * T, -1)
        ra, s_new = _ret_prompt(s3(q), s3(k), s3(v), s3(srg), l, W, ret_tabs, tr)
        om = _att_prompt(s3(qm), s3(km), s3(vm), tq)
        xp = _out_stage(xp, ca, ra.reshape(B * T, -1), om.reshape(B * T, -1), gates, pp3, l, W, tm)
        outs[0].append(s3(ckv)); outs[1].append(s3(kpe)); outs[4].append(s_new)
        outs[6].append(u3[:, T - (width - 1):])

        u, q, k, v, srg, qm, km, vm, ckv, kpe, gates = _in_proj(xs, tabs_s, l, W, offs, Bs, 1)
        ca = _conv_sample(state_conv, u, l, W)
        ra, s_new = _ret_sample(q, k, v, srg, state_ret, l, W, g1)
        om = _att_sample(qm, km, ckv, cache_ckv, cache_kpe_t, page_table, l, W, pp)
        xs = _out_stage(xs, ca, ra, om, gates, ps3, l, W, Bs)
        outs[2].append(ckv[:, None]); outs[3].append(kpe[:, None]); outs[5].append(s_new)
        outs[7].append(jnp.concatenate([state_conv[l][:, 1:], u[:, None]], axis=1))

    st = [jnp.stack(o) for o in outs]
    return (xp.reshape(B, T, D), xs.reshape(Bs, Ts, D), st[0], st[1], st[2], st[3], st[4], st[5], st[6], st[7])
```

```python
import functools

import jax
import jax.numpy as jnp
import numpy as np
from jax import lax
from jax.experimental import pallas as pl
from jax.experimental.pallas import tpu as pltpu

f32 = jnp.float32
bf16 = jnp.bfloat16

EPS = 1e-6
ROPE_BASE = 10000.0
RET_HEADS = 4
RET_DK = 128
RET_DV = 256
RET_CHUNK = 128
MLA_HEADS = 8
MLA_NOPE = 64
MLA_ROPE = 32
MLA_V = 64
MLA_QK = MLA_NOPE + MLA_ROPE
LANE = 128
SUBLANES = 8
HALO_ROWS = 32
VMEM_LIMIT = 56 * 1024 * 1024
MASK_VALUE = -1e30
LOG2_E = 1.4426950408889634
ROW_TILE = 256
ATT_TILE = 512
RET_TILE = 512
PAGES_PER_STEP = 32

_NT = (((1,), (1,)), ((), ()))
_TN = (((0,), (0,)), ((), ()))


def _rms(x, g):
    return x * lax.rsqrt(jnp.mean(x * x, axis=-1, keepdims=True) + EPS) * g


def _const_spec(shape, index):
    return pl.BlockSpec(shape, index, pipeline_mode=pl.Buffered(1))


def _params(sem):
    return pltpu.CompilerParams(dimension_semantics=sem, vmem_limit_bytes=VMEM_LIMIT)


def _mla_rope(x, cm, s1, s2):
    return x * cm + pltpu.roll(x, MLA_ROPE // 2, 1) * s1 + pltpu.roll(x, LANE - MLA_ROPE // 2, 1) * s2


def _in_proj_kernel(*refs, offs, conv_blocks):
    (x_ref, c2r_ref, s2r_ref, cm_ref, s1m_ref, s2m_ref, gmix_ref, win_ref, gqa_ref, wuq_ref,
     gqn_ref, gkva_ref, wuk_ref, wuv_ref, gkn_ref) = refs[:15]
    refs = refs[15:]
    if conv_blocks is not None:
        wdw_ref, bdw_ref, gcln_ref, bcln_ref = refs[:4]
        refs = refs[4:]
    u_ref, q_ref, k_ref, v_ref, srg_ref, qm_ref, km_ref, vm_ref, ckv_ref, kpe_ref, gates_ref = refs[:11]
    refs = refs[11:]
    o_glu, o_rq, o_rk, o_rv, o_rg, o_qa, o_kr, o_kva, o_gates, o_end = offs
    h = _rms(x_ref[...], gmix_ref[...]).astype(bf16)

    def proj(c0, c1):
        return jnp.dot(h, win_ref[:, c0:c1], preferred_element_type=f32)

    glu = proj(o_glu, o_rq)
    half = (o_rq - o_glu) // 2
    u = glu[:, :half] * jax.nn.sigmoid(glu[:, half:])
    u_ref[...] = u

    if conv_blocks is not None:
        ca_ref, carry, ubuf, sbuf = refs
        i = pl.program_id(0)

        @pl.when(i == 0)
        def _():
            carry[...] = jnp.zeros_like(carry)

        tm = u.shape[0]
        ubuf[0:HALO_ROWS, :] = jnp.where(i % conv_blocks > 0, carry[...], 0.0)
        ubuf[HALO_ROWS:, :] = u
        carry[...] = u[tm - HALO_ROWS:, :]
        for step in _conv_ln_swish_steps(ubuf, sbuf, wdw_ref, bdw_ref, gcln_ref, bcln_ref, ca_ref, tm,
                                         wdw_ref.shape[0], tm // 4):
            step()

    c2, s2 = c2r_ref[...], s2r_ref[...]
    rq = proj(o_rq, o_rk)
    rk = proj(o_rk, o_rv)
    for hh in range(RET_HEADS):
        sl = slice(hh * RET_DK, (hh + 1) * RET_DK)
        xq, xk = rq[:, sl], rk[:, sl]
        q_ref[:, sl] = (xq * c2 + pltpu.roll(xq, RET_DK // 2, 1) * s2).astype(bf16)
        k_ref[:, sl] = ((xk * c2 + pltpu.roll(xk, RET_DK // 2, 1) * s2) * (RET_DK ** -0.5)).astype(bf16)
    v_ref[...] = proj(o_rv, o_rg).astype(bf16)
    srg_ref[...] = jax.nn.silu(proj(o_rg, o_qa)).astype(bf16)

    cm, s1, s2m = cm_ref[...], s1m_ref[...], s2m_ref[...]
    qa_kr = proj(o_qa, o_kva)
    c_q = _rms(qa_kr[:, :o_kr - o_qa], gqa_ref[...]).astype(bf16)
    qm = jnp.dot(c_q, wuq_ref[...], preferred_element_type=f32)
    gqn = gqn_ref[...]
    for hh in range(MLA_HEADS):
        sl = slice(hh * LANE, (hh + 1) * LANE)
        r = _mla_rope(qm[:, sl], cm, s1, s2m)
        ss = jnp.sum(r * r, axis=-1, keepdims=True)
        qm_ref[:, sl] = (r * lax.rsqrt(ss / MLA_QK + EPS) * gqn).astype(bf16)

    ckv = _rms(proj(o_kva, o_gates), gkva_ref[...])
    ckv_ref[...] = ckv
    cb = ckv.astype(bf16)
    kpe_slab = _mla_rope(qa_kr[:, o_kr - o_qa:], cm, s1, s2m)
    kpe_ref[...] = kpe_slab[:, MLA_NOPE:MLA_QK]
    kn = jnp.dot(cb, wuk_ref[...], preferred_element_type=f32)
    gkn = gkn_ref[...]
    for hh in range(MLA_HEADS):
        sl = slice(hh * LANE, (hh + 1) * LANE)
        kk = kn[:, sl] + kpe_slab
        ss = jnp.sum(kk * kk, axis=-1, keepdims=True)
        km_ref[:, sl] = (kk * lax.rsqrt(ss / MLA_QK + EPS) * gkn).astype(bf16)
    vm_ref[...] = jnp.dot(cb, wuv_ref[...], preferred_element_type=f32).astype(bf16)

    gates_ref[...] = jax.nn.sigmoid(proj(o_gates, o_end)).astype(bf16)

def _in_proj(x2d, tabs, l, W, offs, tm, n_pos_blocks, fuse_conv):
    M, D = x2d.shape
    conv_dim = (offs[1] - offs[0]) // 2
    n_gates = offs[9] - offs[8]
    kv_lora = offs[8] - offs[7]
    rows = lambda n: pl.BlockSpec((tm, n), lambda i: (i, 0))
    tab = pl.BlockSpec((tm, LANE), lambda i: (i % n_pos_blocks, 0))
    wl = lambda a: _const_spec((None,) + a.shape[1:], lambda i: (l,) + (0,) * (a.ndim - 1))
    out_dims = [(conv_dim, f32), (RET_HEADS * RET_DK, bf16), (RET_HEADS * RET_DK, bf16), (RET_HEADS * RET_DV, bf16),
                (RET_HEADS * RET_DV, bf16), (MLA_HEADS * LANE, bf16), (MLA_HEADS * LANE, bf16),
                (MLA_HEADS * MLA_V, bf16), (kv_lora, f32), (MLA_ROPE, f32), (n_gates, bf16)]
    weights = [W['g_mix'], W['w_in_p'], W['g_qa'], W['w_uq_p'], W['g_qn_p'], W['g_kva'], W['w_uk_p'], W['w_uv_b'],
               W['g_kn_p']]
    scratch = []
    if fuse_conv:
        assert tm >= HALO_ROWS
        weights += [W['w_dw'], W['b_dw'], W['g_cln'], W['b_cln']]
        out_dims.append((conv_dim, bf16))
        scratch = [pltpu.VMEM((HALO_ROWS, conv_dim), f32), pltpu.VMEM((HALO_ROWS + tm, conv_dim), f32),
                   pltpu.VMEM((SUBLANES, HALO_ROWS + tm, conv_dim), f32)]
    return pl.pallas_call(
        functools.partial(_in_proj_kernel, offs=offs, conv_blocks=n_pos_blocks if fuse_conv else None),
        grid=(M // tm,),
        in_specs=[rows(D)] + [tab] * 5 + [wl(a) for a in weights],
        out_specs=[rows(n) for n, _ in out_dims],
        out_shape=[jax.ShapeDtypeStruct((M, n), dt) for n, dt in out_dims],
        scratch_shapes=scratch,
        compiler_params=_params(("arbitrary",) if fuse_conv else ("parallel",)),
        name="in_proj",
    )(x2d, *tabs, *weights)


def _ln_swish(y, g, b):
    mu = jnp.mean(y, axis=-1, keepdims=True)
    d = y - mu
    var = jnp.mean(d * d, axis=-1, keepdims=True)
    z = d * lax.rsqrt(var + EPS) * g + b
    return z * jax.nn.sigmoid(z)


def _conv_ln_swish_steps(ubuf, sbuf, w_ref, b_ref, g_ref, bl_ref, o_ref, tc, width, rc):
    first = HALO_ROWS - (width - 1)
    C = ubuf.shape[-1]
    R = HALO_ROWS + tc

    def shifted_copies():
        sbuf[0] = ubuf[...]
        for ph in range(1, SUBLANES):
            sbuf[ph, 0:R - SUBLANES, :] = ubuf[ph:R - SUBLANES + ph, :]

    def chunk(r0):
        cols = []
        for c0 in range(0, C, LANE):
            acc = jnp.zeros((rc, LANE), f32) + b_ref[:, c0:c0 + LANE]
            for j in range(width):
                ph = (first + j) % SUBLANES
                base = first + j - ph + r0
                acc = acc + w_ref[j:j + 1, c0:c0 + LANE] * sbuf[ph, base:base + rc, c0:c0 + LANE]
            cols.append(acc)
        y = jnp.concatenate(cols, axis=-1)
        o_ref[r0:r0 + rc, :] = _ln_swish(y, g_ref[...], bl_ref[...]).astype(bf16)

    return [shifted_copies] + [functools.partial(chunk, r0) for r0 in range(0, tc, rc)]


def _conv_sample_kernel(st_ref, u_ref, w_ref, b_ref, g_ref, bl_ref, o_ref, *, width):
    y = u_ref[...] * w_ref[width - 1:width, :] + b_ref[...]
    for j in range(width - 1):
        y = y + st_ref[:, j, :] * w_ref[j:j + 1, :]
    o_ref[...] = _ln_swish(y, g_ref[...], bl_ref[...]).astype(bf16)


def _conv_sample(state, u2, l, W):
    Bs, C = u2.shape
    width = W['w_dw'].shape[1]
    wl = lambda a: pl.BlockSpec((None,) + a.shape[1:], lambda i: (l,) + (0,) * (a.ndim - 1))
    weights = [W['w_dw'], W['b_dw'], W['g_cln'], W['b_cln']]
    return pl.pallas_call(
        functools.partial(_conv_sample_kernel, width=width),
        grid=(1,),
        in_specs=[pl.BlockSpec((None,) + state.shape[1:], lambda i: (l, 0, 0, 0)),
                  pl.BlockSpec(u2.shape, lambda i: (0, 0))] + [wl(a) for a in weights],
        out_specs=pl.BlockSpec((Bs, C), lambda i: (0, 0)),
        out_shape=jax.ShapeDtypeStruct((Bs, C), bf16),
        compiler_params=_params(("arbitrary",)),
        name="conv_sample",
    )(state, u2, *weights)


def _group_norm_gate(o, g, gate):
    mu = jnp.mean(o, axis=-1, keepdims=True)
    d = o - mu
    var = jnp.mean(d * d, axis=-1, keepdims=True)
    return (d * lax.rsqrt(var + EPS) * g * gate).astype(bf16)


def _ret_prompt_kernel(gc_ref, q_ref, k_ref, v_ref, srg_ref, decay_ref, cross_ref, kdec_ref, g_ref,
                       o_ref, s_out_ref, s_ref):
    c = pl.program_id(1)

    @pl.when(c == 0)
    def _():
        s_ref[...] = jnp.zeros_like(s_ref)

    C = RET_CHUNK
    for hh in range(RET_HEADS):
        ks = slice(hh * RET_DK, (hh + 1) * RET_DK)
        vs = slice(hh * RET_DV, (hh + 1) * RET_DV)
        cr = jnp.concatenate([cross_ref[hh]] * (RET_DV // LANE), axis=-1)
        s_cur = s_ref[hh]
        for r0 in range(0, q_ref.shape[1], C):
            rows = slice(r0, r0 + C)
            q, k, v = q_ref[0, rows, ks], k_ref[0, rows, ks], v_ref[0, rows, vs]
            scores = lax.dot_general(q, k, _NT, preferred_element_type=f32) * decay_ref[hh]
            o = jnp.dot(scores.astype(bf16), v, preferred_element_type=f32)
            o = o + jnp.dot(q, s_cur.astype(bf16), preferred_element_type=f32) * cr
            kd = (k.astype(f32) * kdec_ref[hh]).astype(bf16)
            s_cur = gc_ref[hh] * s_cur + lax.dot_general(kd, v, _TN, preferred_element_type=f32)
            o_ref[0, rows, vs] = _group_norm_gate(o, g_ref[:, vs], srg_ref[0, rows, vs].astype(f32))
        s_ref[hh] = s_cur

    @pl.when(c == pl.num_programs(1) - 1)
    def _():
        s_out_ref[0] = s_ref[...]


def _ret_prompt(q3, k3, v3, srg3, l, W, tabs, tr):
    B, T, _ = q3.shape
    gc, decay, cross, kdec = tabs
    blk = lambda n: pl.BlockSpec((1, tr, n), lambda b, c: (b, c, 0))
    tab = _const_spec((RET_HEADS, RET_CHUNK, LANE), lambda b, c: (0, 0, 0))
    return pl.pallas_call(
        _ret_prompt_kernel,
        grid=(B, T // tr),
        in_specs=[pl.BlockSpec(memory_space=pltpu.SMEM), blk(RET_HEADS * RET_DK), blk(RET_HEADS * RET_DK),
                  blk(RET_HEADS * RET_DV), blk(RET_HEADS * RET_DV), tab, tab, tab,
                  _const_spec((None, 1, RET_HEADS * RET_DV), lambda b, c: (l, 0, 0))],
        out_specs=[blk(RET_HEADS * RET_DV),
                   pl.BlockSpec((1, RET_HEADS, RET_DK, RET_DV), lambda b, c: (b, 0, 0, 0))],
        out_shape=[jax.ShapeDtypeStruct((B, T, RET_HEADS * RET_DV), bf16),
                   jax.ShapeDtypeStruct((B, RET_HEADS, RET_DK, RET_DV), f32)],
        scratch_shapes=[pltpu.VMEM((RET_HEADS, RET_DK, RET_DV), f32)],
        compiler_params=_params(("parallel", "arbitrary")),
        name="ret_prompt",
    )(gc, q3, k3, v3, srg3, decay, cross, kdec, W['g_rgn'])


def _row_to_col(row, n):
    eye = lax.broadcasted_iota(jnp.int32, (n, n), 0) == lax.broadcasted_iota(jnp.int32, (n, n), 1)
    return jnp.sum(jnp.where(eye, jnp.broadcast_to(row, (n, n)), 0.0), axis=-1, keepdims=True)


def _ret_sample_kernel(g1_ref, q_ref, k_ref, v_ref, srg_ref, s_ref, g_ref, o_ref, s_out_ref):
    for i in range(q_ref.shape[0]):
        for hh in range(RET_HEADS):
            ks = slice(hh * RET_DK, (hh + 1) * RET_DK)
            vs = slice(hh * RET_DV, (hh + 1) * RET_DV)
            qcol = _row_to_col(q_ref[i, :, ks].astype(f32), RET_DK)
            kcol = _row_to_col(k_ref[i, :, ks].astype(f32), RET_DK)
            v = v_ref[i, :, vs].astype(f32)
            s_new = g1_ref[hh] * s_ref[i, hh] + kcol * v
            s_out_ref[i, hh] = s_new
            o = jnp.sum(qcol * s_new, axis=0, keepdims=True)
            o_ref[i, :, vs] = _group_norm_gate(o, g_ref[:, vs], srg_ref[i, :, vs].astype(f32))


def _ret_sample(q2, k2, v2, srg2, state, l, W, g1):
    Bs = q2.shape[0]
    nb = next(c for c in (4, 2, 1) if Bs % c == 0)
    r3 = lambda a: a.reshape(Bs, 1, a.shape[-1])
    row = lambda n: pl.BlockSpec((nb, 1, n), lambda b: (b, 0, 0))
    sblk = (nb, RET_HEADS, RET_DK, RET_DV)
    o, s_new = pl.pallas_call(
        _ret_sample_kernel,
        grid=(Bs // nb,),
        in_specs=[pl.BlockSpec(memory_space=pltpu.SMEM), row(RET_HEADS * RET_DK), row(RET_HEADS * RET_DK),
                  row(RET_HEADS * RET_DV), row(RET_HEADS * RET_DV),
                  pl.BlockSpec((None,) + sblk, lambda b: (l, b, 0, 0, 0)),
                  pl.BlockSpec((None, 1, RET_HEADS * RET_DV), lambda b: (l, 0, 0))],
        out_specs=[row(RET_HEADS * RET_DV), pl.BlockSpec(sblk, lambda b: (b, 0, 0, 0))],
        out_shape=[jax.ShapeDtypeStruct((Bs, 1, RET_HEADS * RET_DV), bf16),
                   jax.ShapeDtypeStruct((Bs, RET_HEADS, RET_DK, RET_DV), f32)],
        compiler_params=_params(("parallel",)),
        name="ret_sample",
    )(g1, r3(q2), r3(k2), r3(v2), r3(srg2), state, W['g_rgn'])
    return o.reshape(Bs, -1), s_new


def _att_prompt_kernel(qi_ref, ki_ref, q_ref, k_ref, v_ref, o_ref, m_ref, l_ref, acc_ref):
    s_idx = pl.program_id(1)
    qi, ki = qi_ref[s_idx], ki_ref[s_idx]
    tq, tk = q_ref.shape[1], k_ref.shape[1]
    n_lt = tk // LANE

    @pl.when(ki == 0)
    def _():
        m_ref[...] = jnp.full_like(m_ref, MASK_VALUE)
        l_ref[...] = jnp.zeros_like(l_ref)
        acc_ref[...] = jnp.zeros_like(acc_ref)

    def step(diagonal):
        for hd in range(MLA_HEADS):
            v = v_ref[0, :, (hd // 2) * LANE:(hd // 2 + 1) * LANE]
            sl = slice(hd * LANE, (hd + 1) * LANE)
            s = lax.dot_general(q_ref[0, :, sl], k_ref[0, :, sl], _NT, preferred_element_type=f32)
            if diagonal:
                row = lax.broadcasted_iota(jnp.int32, (tq, tk), 0)
                col = lax.broadcasted_iota(jnp.int32, (tq, tk), 1)
                s = jnp.where(col <= row, s, MASK_VALUE)
            m_prev = m_ref[hd]
            smax = s[:, 0:LANE]
            for c in range(1, n_lt):
                smax = jnp.maximum(smax, s[:, c * LANE:(c + 1) * LANE])
            m_new = jnp.maximum(m_prev, jnp.max(smax, axis=-1, keepdims=True))
            a = jnp.exp2(m_prev - m_new)
            p = jnp.exp2(s - jnp.concatenate([m_new] * n_lt, axis=-1))
            psum = p[:, 0:LANE]
            for c in range(1, n_lt):
                psum = psum + p[:, c * LANE:(c + 1) * LANE]
            l_ref[hd] = a * l_ref[hd] + psum
            acc_ref[hd] = a * acc_ref[hd] + jnp.dot(p.astype(bf16), v, preferred_element_type=f32)
            m_ref[hd] = m_new

    @pl.when(ki < qi)
    def _():
        step(False)

    @pl.when(ki == qi)
    def _():
        step(True)
        lane = lax.broadcasted_iota(jnp.int32, (tq, LANE), 1)
        for g in range(MLA_HEADS // 2):
            l0 = jnp.sum(l_ref[2 * g], axis=-1, keepdims=True)
            l1 = jnp.sum(l_ref[2 * g + 1], axis=-1, keepdims=True)
            o = jnp.where(lane < MLA_V, acc_ref[2 * g] / l0, acc_ref[2 * g + 1] / l1)
            o_ref[0, :, g * LANE:(g + 1) * LANE] = o.astype(bf16)


def _att_prompt(qm3, km3, vm3, tq):
    B, T, _ = qm3.shape
    nq = T // tq
    H = MLA_HEADS
    pairs = [(qi, ki) for qi in range(nq) for ki in range(qi + 1)]
    qi_tab = jnp.asarray([p[0] for p in pairs], jnp.int32)
    ki_tab = jnp.asarray([p[1] for p in pairs], jnp.int32)
    grid_spec = pltpu.PrefetchScalarGridSpec(
        num_scalar_prefetch=2,
        grid=(B, len(pairs)),
        in_specs=[pl.BlockSpec((1, tq, H * LANE), lambda b, s, qi, ki: (b, qi[s], 0)),
                  pl.BlockSpec((1, tq, H * LANE), lambda b, s, qi, ki: (b, ki[s], 0)),
                  pl.BlockSpec((1, tq, H * MLA_V), lambda b, s, qi, ki: (b, ki[s], 0))],
        out_specs=pl.BlockSpec((1, tq, H * MLA_V), lambda b, s, qi, ki: (b, qi[s], 0)),
        scratch_shapes=[pltpu.VMEM((H, tq, LANE), f32), pltpu.VMEM((H, tq, LANE), f32),
                        pltpu.VMEM((H, tq, LANE), f32)],
    )
    return pl.pallas_call(
        _att_prompt_kernel,
        grid_spec=grid_spec,
        out_shape=jax.ShapeDtypeStruct((B, T, H * MLA_V), bf16),
        compiler_params=_params(("parallel", "arbitrary")),
        name="att_prompt",
    )(qi_tab, ki_tab, qm3, km3, vm3)


def _hi_lo(x):
    hi = x.astype(bf16).astype(f32)
    return jnp.concatenate([hi, x - hi], axis=0).astype(bf16)


def _rows_select(rows, n=None):
    n, w = n or len(rows), rows[0].shape[-1]
    sub = lax.broadcasted_iota(jnp.int32, (n, w), 0)
    out = jnp.zeros((n, w), rows[0].dtype)
    for i, r in enumerate(rows):
        out = jnp.where(sub == i, jnp.broadcast_to(r, (n, w)), out)
    return out


def _att_sample_kernel(pt_ref, ckv_hbm, kpe_hbm, qm_ref, kmn_ref, ckvn_ref, gkn_ref, wukt_ref, wuktp_ref, wuv_ref,
                       o_ref, ckv_buf, kpe_buf, sem_c, sem_k, aq_ref, cbb_ref, *, l, pp, nj):
    b = pl.program_id(0)
    nb = pl.num_programs(0)
    H = MLA_HEADS
    page = ckv_hbm.shape[2]

    def copies(bb, j, slot):
        out = []
        for i in range(pp):
            pg = pt_ref[bb, j * pp + i]
            out.append(pltpu.make_async_copy(ckv_hbm.at[l, pg], ckv_buf.at[slot, pl.ds(i * page, page)],
                                             sem_c.at[slot]))
            out.append(pltpu.make_async_copy(kpe_hbm.at[l, pg], kpe_buf.at[slot, :, pl.ds(i * page, page)],
                                             sem_k.at[slot]))
        return out

    @pl.when(b == 0)
    def _():
        for c in copies(0, 0, 0) + copies(0, 1, 1):
            c.start()

    qg = qm_ref[0].astype(f32) * jnp.concatenate([gkn_ref[...]] * H, axis=-1)
    sub = lax.broadcasted_iota(jnp.int32, (2 * H, H * LANE), 0)
    head = lax.broadcasted_iota(jnp.int32, (2 * H, H * LANE), 1) // LANE
    qbd = jnp.where(sub == head, jnp.broadcast_to(qg, (2 * H, H * LANE)), 0.0).astype(bf16)
    qabs = jnp.dot(qbd, wuktp_ref[...], preferred_element_type=f32)[0:H]
    aq_ref[0:H * MLA_NOPE, :] = wukt_ref[...]
    aq_ref[H * MLA_NOPE:, :] = _hi_lo(qabs)
    qr = _rows_select([qg[:, h * LANE + MLA_NOPE:h * LANE + MLA_QK] for h in range(H)], 2 * H).astype(bf16)

    def scores(j):
        slot = j % 2
        for c in copies(b, j, slot):
            c.wait()
        cbb_ref[slot] = ckv_buf[slot].astype(bf16)
        cb = cbb_ref[slot]
        kt = kpe_buf[slot]
        r = lax.dot_general(aq_ref[...], cb, _NT, preferred_element_type=f32)
        ss = jnp.concatenate(
            [jnp.sum(jnp.square(r[h * MLA_NOPE:(h + 1) * MLA_NOPE]), axis=0, keepdims=True) for h in range(H)],
            axis=0)
        ss_rope = jnp.sum(kt * kt, axis=0, keepdims=True)
        s_rope = jnp.dot(qr, kt.astype(bf16), preferred_element_type=f32)[0:H]
        s_nope = r[H * MLA_NOPE:H * MLA_NOPE + H] + r[H * MLA_NOPE + H:]
        s = (s_nope + s_rope) * lax.rsqrt((ss + ss_rope) / MLA_QK + EPS)
        if j + 2 < nj:
            for c in copies(b, j + 2, slot):
                c.start()
        else:
            @pl.when(b + 1 < nb)
            def _():
                for c in copies(b + 1, j + 2 - nj, slot):
                    c.start()
        return s

    def accumulate(j, s, m, lsum, ctx):
        m_new = jnp.maximum(m, jnp.max(s, axis=-1, keepdims=True))
        a = jnp.exp2(m - m_new)
        p = jnp.exp2(s - m_new)
        lsum = a * lsum + jnp.sum(p, axis=-1, keepdims=True)
        p16 = jnp.concatenate([p, jnp.zeros_like(p)], axis=0).astype(bf16)
        ctx = a * ctx + jnp.dot(p16, cbb_ref[j % 2], preferred_element_type=f32)[0:H]
        return m_new, lsum, ctx

    m = jnp.full((H, 1), MASK_VALUE, f32)
    lsum = jnp.zeros((H, 1), f32)
    ctx = jnp.zeros((H, ckvn_ref.shape[-1]), f32)
    s_prev = scores(0)
    for j in range(1, nj):
        s_cur = scores(j)
        m, lsum, ctx = accumulate(j - 1, s_prev, m, lsum, ctx)
        s_prev = s_cur
    m, lsum, ctx = accumulate(nj - 1, s_prev, m, lsum, ctx)

    prod = qm_ref[0].astype(f32) * kmn_ref[0].astype(f32)
    s_new = jnp.sum(_rows_select([prod[:, h * LANE:(h + 1) * LANE] for h in range(H)]), axis=-1, keepdims=True)
    m_fin = jnp.maximum(m, s_new)
    a = jnp.exp2(m - m_fin)
    pn = jnp.exp2(s_new - m_fin)
    l_fin = a * lsum + pn
    c_new = ckvn_ref[0].astype(bf16).astype(f32)
    ctx = (a * ctx + pn.astype(bf16).astype(f32) * c_new) / l_fin
    both = jnp.dot(_hi_lo(ctx), wuv_ref[...], preferred_element_type=f32)
    full = both[0:H] + both[H:]
    sub = lax.broadcasted_iota(jnp.int32, full.shape, 0)
    head = lax.broadcasted_iota(jnp.int32, full.shape, 1) // MLA_V
    o_ref[0] = jnp.sum(jnp.where(sub == head, full, 0.0), axis=0, keepdims=True).astype(bf16)


def _att_sample(qm2, kmn2, ckvn2, cache_ckv, cache_kpe_t, page_table, l, W, pp):
    Bs = qm2.shape[0]
    n_pages = page_table.shape[1]
    page, n_lat = cache_ckv.shape[2], cache_ckv.shape[3]
    rope = cache_kpe_t.shape[2]
    H = MLA_HEADS
    nj = n_pages // pp
    assert nj % 2 == 0, "the two staging slots alternate by step parity across requests"
    r3 = lambda a: a.reshape(Bs, 1, a.shape[-1])
    row = lambda n: pl.BlockSpec((1, 1, n), lambda b, pt: (b, 0, 0))
    wl = lambda a: _const_spec((None,) + a.shape[1:], lambda b, pt: (l,) + (0,) * (a.ndim - 1))
    weights = [W['g_kn_p'], W['w_ukt_b'], W['w_ukt_p'], W['w_uv_b']]
    grid_spec = pltpu.PrefetchScalarGridSpec(
        num_scalar_prefetch=1,
        grid=(Bs,),
        in_specs=([pl.BlockSpec(memory_space=pl.ANY), pl.BlockSpec(memory_space=pl.ANY)]
                  + [row(H * LANE), row(H * LANE), row(n_lat)] + [wl(a) for a in weights]),
        out_specs=row(H * MLA_V),
        scratch_shapes=[pltpu.VMEM((2, pp * page, n_lat), f32), pltpu.VMEM((2, rope, pp * page), f32),
                        pltpu.SemaphoreType.DMA((2,)), pltpu.SemaphoreType.DMA((2,)),
                        pltpu.VMEM((H * MLA_NOPE + 2 * H, n_lat), bf16), pltpu.VMEM((2, pp * page, n_lat), bf16)],
    )
    o = pl.pallas_call(
        functools.partial(_att_sample_kernel, l=l, pp=pp, nj=nj),
        grid_spec=grid_spec,
        out_shape=jax.ShapeDtypeStruct((Bs, 1, H * MLA_V), bf16),
        compiler_params=_params(("arbitrary",)),
        name="att_sample",
    )(page_table, cache_ckv, cache_kpe_t, r3(qm2), r3(kmn2), r3(ckvn2), *weights)
    return o.reshape(Bs, -1)


def _out_kernel(x_ref, ca_ref, ra_ref, om_ref, gates_ref, p_ref, wco_ref, wro_ref, wmo_ref, wout_ref, gffn_ref,
                wg_ref, wu_ref, wd_ref, gple_ref, wpg_ref, wpp_ref, o_ref):
    D = x_ref.shape[-1]
    dot = lambda a, b: jnp.dot(a, b, preferred_element_type=f32)
    gate = lambda i: gates_ref[:, i * D:(i + 1) * D].astype(f32)
    mix = (gate(0) * dot(ca_ref[...], wco_ref[...]) + gate(1) * dot(ra_ref[...], wro_ref[...])
           + gate(2) * dot(om_ref[...], wmo_ref[...]))
    x = x_ref[...] + dot(mix.astype(bf16), wout_ref[...])
    hf = _rms(x, gffn_ref[...]).astype(bf16)
    ff = jax.nn.silu(dot(hf, wg_ref[...])) * dot(hf, wu_ref[...])
    x = x + dot(ff.astype(bf16), wd_ref[...])
    hp = _rms(x, gple_ref[...]).astype(bf16)
    pg = jax.nn.sigmoid(dot(hp, wpg_ref[...]))
    o_ref[...] = x + pg * dot(p_ref[...].astype(bf16), wpp_ref[...])


def _out_stage(x2d, ca, ra, om, gates, p3, l, W, tm):
    M, D = x2d.shape
    rows = lambda n: pl.BlockSpec((tm, n), lambda i: (i, 0))
    wl = lambda a: _const_spec((None,) + a.shape[1:], lambda i: (l,) + (0,) * (a.ndim - 1))
    weights = [W['w_conv_out_b'], W['w_ret_out_b'], W['w_mla_out_b'], W['w_out_b'], W['g_ffn'], W['w_gate_b'],
               W['w_up_b'], W['w_down_b'], W['g_ple'], W['w_ple_gate_b'], W['w_ple_proj_b']]
    return pl.pallas_call(
        _out_kernel,
        grid=(M // tm,),
        in_specs=[rows(D), rows(ca.shape[1]), rows(ra.shape[1]), rows(om.shape[1]), rows(gates.shape[1]),
                  pl.BlockSpec((None, tm, p3.shape[-1]), lambda i: (l, i, 0))] + [wl(a) for a in weights],
        out_specs=rows(D),
        out_shape=jax.ShapeDtypeStruct((M, D), f32),
        compiler_params=_params(("parallel",)),
        name="out_stage",
    )(x2d, ca, ra, om, gates, p3, *weights)


def _rope_tables(pos):
    n = pos.shape[0]
    posf = pos.astype(f32)[:, None]

    def cs(half):
        inv = ROPE_BASE ** (-jnp.arange(half, dtype=f32) / half)
        ang = posf * inv[None, :]
        return jnp.cos(ang), jnp.sin(ang)

    c, s = cs(RET_DK // 2)
    c2r = jnp.concatenate([c, c], axis=-1)
    s2r = jnp.concatenate([-s, s], axis=-1)
    c, s = cs(MLA_ROPE // 2)
    hr = MLA_ROPE // 2
    z = lambda w: jnp.zeros((n, w), f32)
    cm = jnp.concatenate([jnp.ones((n, MLA_NOPE), f32), c, c, z(LANE - MLA_QK)], axis=-1)
    s1m = jnp.concatenate([z(MLA_NOPE + hr), s, z(LANE - MLA_QK)], axis=-1)
    s2m = jnp.concatenate([z(MLA_NOPE), -s, z(LANE - MLA_NOPE - hr)], axis=-1)
    return c2r, s2r, cm, s1m, s2m


def _retention_tables(C):
    log_gamma = jnp.log(1.0 - 2.0 ** (-5.0 - jnp.arange(RET_HEADS, dtype=f32)))
    idx = jnp.arange(C, dtype=f32)
    diff = idx[:, None] - idx[None, :]
    decay = jnp.where(diff >= 0, jnp.exp(jnp.maximum(diff, 0.0)[None] * log_gamma[:, None, None]), 0.0)
    cross = jnp.exp((idx + 1.0)[:, None] * log_gamma[None, :])
    kdec = jnp.exp((C - 1.0 - idx)[:, None] * log_gamma[None, :])
    gc = jnp.exp(C * log_gamma)
    lanes = lambda a: jnp.broadcast_to(a.T[:, :, None], (RET_HEADS, C, LANE))
    return gc, decay, lanes(cross), lanes(kdec)


def _pad_heads(w, heads, width):
    w = w.reshape(w.shape[:-1] + (heads, width))
    w = jnp.pad(w, [(0, 0)] * (w.ndim - 1) + [(0, LANE - width)])
    return w.reshape(w.shape[:-2] + (heads * LANE,))


def kernel(x_prompt, x_sample, p_prompt, p_sample, cache_ckv, cache_kpe, state_ret, state_conv, page_table, g_mix, w_in, w_dw, b_dw, g_cln, b_cln, w_conv_out, g_rgn, w_ret_out, g_qa, w_uq, g_qn, g_kva, w_uk, w_uv, g_kn, w_mla_out, w_out, g_ffn, w_gate, w_up, w_down, g_ple, w_ple_gate, w_ple_proj):
    B, T, D = x_prompt.shape
    Bs, Ts, _ = x_sample.shape
    assert Ts == 1, "the sample group decodes one token per request"
    depth = w_in.shape[0]
    conv_dim, width = w_dw.shape[2], w_dw.shape[1]
    q_lora, kv_lora = g_qa.shape[1], g_kva.shape[1]
    page = cache_ckv.shape[2]
    past_len = page_table.shape[1] * page
    assert T % RET_CHUNK == 0 and width - 1 <= HALO_ROWS <= T

    sizes = (2 * conv_dim, RET_HEADS * RET_DK, RET_HEADS * RET_DK, RET_HEADS * RET_DV, RET_HEADS * RET_DV,
             q_lora, kv_lora, MLA_ROPE, 3 * D)
    src = np.concatenate([[0], np.cumsum(sizes)]).tolist()
    kr_slab = jnp.pad(w_in[..., src[7]:src[8]], [(0, 0), (0, 0), (MLA_NOPE, LANE - MLA_QK)])
    w_in_p = jnp.concatenate([w_in[..., :src[6]], kr_slab, w_in[..., src[6]:src[7]], w_in[..., src[8]:]],
                             axis=-1).astype(bf16)
    offs = tuple(src[:7]) + (src[6] + LANE, src[7] + LANE, src[7] + LANE + 3 * D)

    row = lambda g: g[:, None, :]
    lane_pad = lambda g: jnp.pad(g, [(0, 0), (0, LANE - g.shape[-1])])[:, None, :]
    W = dict(
        g_mix=row(g_mix), w_in_p=w_in_p, g_qa=row(g_qa), g_kva=row(g_kva),
        w_uq_p=_pad_heads(w_uq, MLA_HEADS, MLA_QK).astype(bf16),
        g_qn_p=lane_pad(g_qn * (MLA_QK ** -0.5 * LOG2_E)),
        w_uk_p=_pad_heads(w_uk, MLA_HEADS, MLA_NOPE).astype(bf16),
        g_kn_p=lane_pad(g_kn),
        w_uv_b=w_uv.astype(bf16),
        w_ukt_b=jnp.swapaxes(w_uk, 1, 2).astype(bf16),
        w_ukt_p=jnp.swapaxes(_pad_heads(w_uk, MLA_HEADS, MLA_NOPE), 1, 2).astype(bf16),
        w_dw=w_dw, b_dw=row(b_dw), g_cln=row(g_cln), b_cln=row(b_cln), g_rgn=row(g_rgn),
        w_conv_out_b=w_conv_out.astype(bf16), w_ret_out_b=w_ret_out.astype(bf16),
        w_mla_out_b=w_mla_out.astype(bf16), w_out_b=w_out.astype(bf16), g_ffn=row(g_ffn),
        w_gate_b=w_gate.astype(bf16), w_up_b=w_up.astype(bf16), w_down_b=w_down.astype(bf16),
        g_ple=row(g_ple), w_ple_gate_b=w_ple_gate.astype(bf16), w_ple_proj_b=w_ple_proj.astype(bf16),
    )

    tabs_p = _rope_tables(jnp.arange(T, dtype=jnp.int32))
    tabs_s = _rope_tables(jnp.full((Bs,), past_len, jnp.int32))
    ret_tabs = _retention_tables(RET_CHUNK)
    g1 = _retention_tables(1)[0]

    tm = min(ROW_TILE, T)
    tq = min(ATT_TILE, T)
    tr = min(RET_TILE, T)
    n_pages = page_table.shape[1]
    pp = next(c for c in (PAGES_PER_STEP, 8, 4, 2, 1) if n_pages % (2 * c) == 0)

    cache_kpe_t = jnp.swapaxes(cache_kpe, 2, 3)

    xp = x_prompt.reshape(B * T, D)
    xs = x_sample.reshape(Bs, D)
    pp3 = p_prompt.reshape(depth, B * T, -1)
    ps3 = p_sample.reshape(depth, Bs, -1)
    outs = [[] for _ in range(8)]
    for l in range(depth):
        u, q, k, v, srg, qm, km, vm, ckv, kpe, gates, ca = _in_proj(xp, tabs_p, l, W, offs, tm, T // tm, True)
        s3 = lambda a: a.reshape(B, T, a.shape[-1])
        u3 = s3(u)
        ra, s_new = _ret_prompt(s3(q), s3(k), s3(v), s3(srg), l, W, ret_tabs, tr)
        om = _att_prompt(s3(qm), s3(km), s3(vm), tq)
        xp = _out_stage(xp, ca, ra.reshape(B * T, -1), om.reshape(B * T, -1), gates, pp3, l, W, tm)
        outs[0].append(s3(ckv)); outs[1].append(s3(kpe)); outs[4].append(s_new)
        outs[6].append(u3[:, T - (width - 1):])

        u, q, k, v, srg, qm, km, vm, ckv, kpe, gates = _in_proj(xs, tabs_s, l, W, offs, Bs, 1, False)
        ca = _conv_sample(state_conv, u, l, W)
        ra, s_new = _ret_sample(q, k, v, srg, state_ret, l, W, g1)
        om = _att_sample(qm, km, ckv, cache_ckv, cache_kpe_t, page_table, l, W, pp)
        xs = _out_stage(xs, ca, ra, om, gates, ps3, l, W, Bs)
        outs[2].append(ckv[:, None]); outs[3].append(kpe[:, None]); outs[5].append(s_new)
        outs[7].append(jnp.concatenate([state_conv[l][:, 1:], u[:, None]], axis=1))

    st = [jnp.stack(o) for o in outs]
    return (xp.reshape(B, T, D), xs.reshape(Bs, Ts, D), st[0], st[1], st[2], st[3], st[4], st[5], st[6], st[7])
```

```python
import functools

import jax
import jax.numpy as jnp
import numpy as np
from jax import lax
from jax.experimental import pallas as pl
from jax.experimental.pallas import tpu as pltpu

f32 = jnp.float32
bf16 = jnp.bfloat16

EPS = 1e-6
ROPE_BASE = 10000.0
RET_HEADS = 4
RET_DK = 128
RET_DV = 256
RET_CHUNK = 128
MLA_HEADS = 8
MLA_NOPE = 64
MLA_ROPE = 32
MLA_V = 64
MLA_QK = MLA_NOPE + MLA_ROPE
LANE = 128
SUBLANES = 8
HALO_ROWS = 32
VMEM_LIMIT = 56 * 1024 * 1024
MASK_VALUE = -1e30
LOG2_E = 1.4426950408889634
ROW_TILE = 256
ATT_TILE = 512
RET_TILE = 1024
PAGES_PER_STEP = 32

_NT = (((1,), (1,)), ((), ()))
_TN = (((0,), (0,)), ((), ()))


def _rms(x, g):
    return x * lax.rsqrt(jnp.mean(x * x, axis=-1, keepdims=True) + EPS) * g


def _const_spec(shape, index):
    return pl.BlockSpec(shape, index, pipeline_mode=pl.Buffered(1))


def _params(sem):
    return pltpu.CompilerParams(dimension_semantics=sem, vmem_limit_bytes=VMEM_LIMIT)


def _mla_rope(x, cm, s1, s2):
    return x * cm + pltpu.roll(x, MLA_ROPE // 2, 1) * s1 + pltpu.roll(x, LANE - MLA_ROPE // 2, 1) * s2


def _in_proj_kernel(*refs, offs, conv_blocks):
    (x_ref, c2r_ref, s2r_ref, cm_ref, s1m_ref, s2m_ref, gmix_ref, win_ref, wqk_ref, wkva_ref, wg_ref, gqa_ref,
     wuq_ref, gqn_ref, gkva_ref, wuk_ref, wuv_ref, gkn_ref) = refs[:18]
    refs = refs[18:]
    if conv_blocks is not None:
        wdw_ref, bdw_ref, gcln_ref, bcln_ref = refs[:4]
        refs = refs[4:]
    u_ref, q_ref, k_ref, v_ref, srg_ref, qm_ref, km_ref, vm_ref, ckv_ref, kpe_ref, gates_ref = refs[:11]
    refs = refs[11:]
    o_glu, o_rq, o_rk, o_rv, o_rg, o_qa = offs
    h = _rms(x_ref[...], gmix_ref[...]).astype(bf16)

    def proj(c0, c1):
        return jnp.dot(h, win_ref[:, c0:c1], preferred_element_type=f32)

    def proj_all(w_ref):
        return jnp.dot(h, w_ref[...], preferred_element_type=f32)

    glu = proj(o_glu, o_rq)
    half = (o_rq - o_glu) // 2
    u = glu[:, :half] * jax.nn.sigmoid(glu[:, half:])
    u_ref[...] = u

    if conv_blocks is not None:
        ca_ref, carry, ubuf, sbuf = refs
        i = pl.program_id(0)

        @pl.when(i == 0)
        def _():
            carry[...] = jnp.zeros_like(carry)

        tm = u.shape[0]
        ubuf[0:HALO_ROWS, :] = jnp.where(i % conv_blocks > 0, carry[...], 0.0)
        ubuf[HALO_ROWS:, :] = u
        carry[...] = u[tm - HALO_ROWS:, :]
        for step in _conv_ln_swish_steps(ubuf, sbuf, wdw_ref, bdw_ref, gcln_ref, bcln_ref, ca_ref, tm,
                                         wdw_ref.shape[0], tm // 4):
            step()

    c2, s2 = c2r_ref[...], s2r_ref[...]
    rq = proj(o_rq, o_rk)
    rk = proj(o_rk, o_rv)
    for hh in range(RET_HEADS):
        sl = slice(hh * RET_DK, (hh + 1) * RET_DK)
        xq, xk = rq[:, sl], rk[:, sl]
        q_ref[:, sl] = (xq * c2 + pltpu.roll(xq, RET_DK // 2, 1) * s2).astype(bf16)
        k_ref[:, sl] = ((xk * c2 + pltpu.roll(xk, RET_DK // 2, 1) * s2) * (RET_DK ** -0.5)).astype(bf16)
    v_ref[...] = proj(o_rv, o_rg).astype(bf16)
    srg_ref[...] = jax.nn.silu(proj(o_rg, o_qa)).astype(bf16)

    cm, s1, s2m = cm_ref[...], s1m_ref[...], s2m_ref[...]
    qa_kr = proj_all(wqk_ref)
    q_lora = gqa_ref.shape[-1]
    c_q = _rms(qa_kr[:, :q_lora], gqa_ref[...]).astype(bf16)
    qm = jnp.dot(c_q, wuq_ref[...], preferred_element_type=f32)
    gqn = gqn_ref[...]
    for hh in range(MLA_HEADS):
        sl = slice(hh * LANE, (hh + 1) * LANE)
        r = _mla_rope(qm[:, sl], cm, s1, s2m)
        ss = jnp.sum(r * r, axis=-1, keepdims=True)
        qm_ref[:, sl] = (r * lax.rsqrt(ss / MLA_QK + EPS) * gqn).astype(bf16)

    ckv = _rms(proj_all(wkva_ref), gkva_ref[...])
    ckv_ref[...] = ckv
    cb = ckv.astype(bf16)
    kpe_slab = _mla_rope(qa_kr[:, q_lora:], cm, s1, s2m)
    kpe_ref[...] = kpe_slab[:, MLA_NOPE:MLA_QK]
    kn = jnp.dot(cb, wuk_ref[...], preferred_element_type=f32)
    gkn = gkn_ref[...]
    for hh in range(MLA_HEADS):
        sl = slice(hh * LANE, (hh + 1) * LANE)
        kk = kn[:, sl] + kpe_slab
        ss = jnp.sum(kk * kk, axis=-1, keepdims=True)
        km_ref[:, sl] = (kk * lax.rsqrt(ss / MLA_QK + EPS) * gkn).astype(bf16)
    vm_ref[...] = jnp.dot(cb, wuv_ref[...], preferred_element_type=f32).astype(bf16)

    gates_ref[...] = jax.nn.sigmoid(proj_all(wg_ref)).astype(bf16)

def _in_proj(x2d, tabs, l, W, offs, tm, n_pos_blocks, fuse_conv):
    M, D = x2d.shape
    conv_dim = (offs[1] - offs[0]) // 2
    n_gates = W['w_in_g'].shape[-1]
    kv_lora = W['w_in_kva'].shape[-1]
    rows = lambda n: pl.BlockSpec((tm, n), lambda i: (i, 0))
    tab = pl.BlockSpec((tm, LANE), lambda i: (i % n_pos_blocks, 0))
    wl = lambda a: _const_spec((None,) + a.shape[1:], lambda i: (l,) + (0,) * (a.ndim - 1))
    out_dims = [(conv_dim, f32), (RET_HEADS * RET_DK, bf16), (RET_HEADS * RET_DK, bf16), (RET_HEADS * RET_DV, bf16),
                (RET_HEADS * RET_DV, bf16), (MLA_HEADS * LANE, bf16), (MLA_HEADS * LANE, bf16),
                (MLA_HEADS * MLA_V, bf16), (kv_lora, f32), (MLA_ROPE, f32), (n_gates, bf16)]
    weights = [W['g_mix'], W['w_in_a'], W['w_in_qk'], W['w_in_kva'], W['w_in_g'], W['g_qa'], W['w_uq_p'],
               W['g_qn_p'], W['g_kva'], W['w_uk_p'], W['w_uv_b'],
               W['g_kn_p']]
    scratch = []
    if fuse_conv:
        assert tm >= HALO_ROWS
        weights += [W['w_dw'], W['b_dw'], W['g_cln'], W['b_cln']]
        out_dims.append((conv_dim, bf16))
        scratch = [pltpu.VMEM((HALO_ROWS, conv_dim), f32), pltpu.VMEM((HALO_ROWS + tm, conv_dim), f32),
                   pltpu.VMEM((SUBLANES, HALO_ROWS + tm, conv_dim), f32)]
    return pl.pallas_call(
        functools.partial(_in_proj_kernel, offs=offs, conv_blocks=n_pos_blocks if fuse_conv else None),
        grid=(M // tm,),
        in_specs=[rows(D)] + [tab] * 5 + [wl(a) for a in weights],
        out_specs=[rows(n) for n, _ in out_dims],
        out_shape=[jax.ShapeDtypeStruct((M, n), dt) for n, dt in out_dims],
        scratch_shapes=scratch,
        compiler_params=_params(("arbitrary",) if fuse_conv else ("parallel",)),
        name="in_proj",
    )(x2d, *tabs, *weights)


def _ln_swish(y, g, b):
    mu = jnp.mean(y, axis=-1, keepdims=True)
    d = y - mu
    var = jnp.mean(d * d, axis=-1, keepdims=True)
    z = d * lax.rsqrt(var + EPS) * g + b
    return z * jax.nn.sigmoid(z)


def _conv_ln_swish_steps(ubuf, sbuf, w_ref, b_ref, g_ref, bl_ref, o_ref, tc, width, rc):
    first = HALO_ROWS - (width - 1)
    C = ubuf.shape[-1]
    R = HALO_ROWS + tc

    def shifted_copies():
        sbuf[0] = ubuf[...]
        for ph in range(1, SUBLANES):
            sbuf[ph, 0:R - SUBLANES, :] = ubuf[ph:R - SUBLANES + ph, :]

    def chunk(r0):
        cols = []
        for c0 in range(0, C, LANE):
            acc = jnp.zeros((rc, LANE), f32) + b_ref[:, c0:c0 + LANE]
            for j in range(width):
                ph = (first + j) % SUBLANES
                base = first + j - ph + r0
                acc = acc + w_ref[j:j + 1, c0:c0 + LANE] * sbuf[ph, base:base + rc, c0:c0 + LANE]
            cols.append(acc)
        y = jnp.concatenate(cols, axis=-1)
        o_ref[r0:r0 + rc, :] = _ln_swish(y, g_ref[...], bl_ref[...]).astype(bf16)

    return [shifted_copies] + [functools.partial(chunk, r0) for r0 in range(0, tc, rc)]


def _conv_sample_kernel(st_ref, u_ref, w_ref, b_ref, g_ref, bl_ref, o_ref, *, width):
    y = u_ref[...] * w_ref[width - 1:width, :] + b_ref[...]
    for j in range(width - 1):
        y = y + st_ref[:, j, :] * w_ref[j:j + 1, :]
    o_ref[...] = _ln_swish(y, g_ref[...], bl_ref[...]).astype(bf16)


def _conv_sample(state, u2, l, W):
    Bs, C = u2.shape
    width = W['w_dw'].shape[1]
    wl = lambda a: pl.BlockSpec((None,) + a.shape[1:], lambda i: (l,) + (0,) * (a.ndim - 1))
    weights = [W['w_dw'], W['b_dw'], W['g_cln'], W['b_cln']]
    return pl.pallas_call(
        functools.partial(_conv_sample_kernel, width=width),
        grid=(1,),
        in_specs=[pl.BlockSpec((None,) + state.shape[1:], lambda i: (l, 0, 0, 0)),
                  pl.BlockSpec(u2.shape, lambda i: (0, 0))] + [wl(a) for a in weights],
        out_specs=pl.BlockSpec((Bs, C), lambda i: (0, 0)),
        out_shape=jax.ShapeDtypeStruct((Bs, C), bf16),
        compiler_params=_params(("arbitrary",)),
        name="conv_sample",
    )(state, u2, *weights)


def _group_norm_gate(o, g, gate):
    mu = jnp.mean(o, axis=-1, keepdims=True)
    d = o - mu
    var = jnp.mean(d * d, axis=-1, keepdims=True)
    return (d * lax.rsqrt(var + EPS) * g * gate).astype(bf16)


def _ret_prompt_kernel(gc_ref, q_ref, k_ref, v_ref, srg_ref, decay_ref, cross_ref, kdec_ref, g_ref,
                       o_ref, s_out_ref, s_ref):
    c = pl.program_id(1)

    @pl.when(c == 0)
    def _():
        s_ref[...] = jnp.zeros_like(s_ref)

    C = RET_CHUNK
    for hh in range(RET_HEADS):
        ks = slice(hh * RET_DK, (hh + 1) * RET_DK)
        vs = slice(hh * RET_DV, (hh + 1) * RET_DV)
        cr = jnp.concatenate([cross_ref[hh]] * (RET_DV // LANE), axis=-1)
        s_cur = s_ref[hh]
        for r0 in range(0, q_ref.shape[1], C):
            rows = slice(r0, r0 + C)
            q, k, v = q_ref[0, rows, ks], k_ref[0, rows, ks], v_ref[0, rows, vs]
            scores = lax.dot_general(q, k, _NT, preferred_element_type=f32) * decay_ref[hh]
            o = jnp.dot(scores.astype(bf16), v, preferred_element_type=f32)
            o = o + jnp.dot(q, s_cur.astype(bf16), preferred_element_type=f32) * cr
            kd = (k.astype(f32) * kdec_ref[hh]).astype(bf16)
            s_cur = gc_ref[hh] * s_cur + lax.dot_general(kd, v, _TN, preferred_element_type=f32)
            o_ref[0, rows, vs] = _group_norm_gate(o, g_ref[:, vs], srg_ref[0, rows, vs].astype(f32))
        s_ref[hh] = s_cur

    @pl.when(c == pl.num_programs(1) - 1)
    def _():
        s_out_ref[0] = s_ref[...]


def _ret_prompt(q3, k3, v3, srg3, l, W, tabs, tr):
    B, T, _ = q3.shape
    gc, decay, cross, kdec = tabs
    blk = lambda n: pl.BlockSpec((1, tr, n), lambda b, c: (b, c, 0))
    tab = _const_spec((RET_HEADS, RET_CHUNK, LANE), lambda b, c: (0, 0, 0))
    return pl.pallas_call(
        _ret_prompt_kernel,
        grid=(B, T // tr),
        in_specs=[pl.BlockSpec(memory_space=pltpu.SMEM), blk(RET_HEADS * RET_DK), blk(RET_HEADS * RET_DK),
                  blk(RET_HEADS * RET_DV), blk(RET_HEADS * RET_DV), tab, tab, tab,
                  _const_spec((None, 1, RET_HEADS * RET_DV), lambda b, c: (l, 0, 0))],
        out_specs=[blk(RET_HEADS * RET_DV),
                   pl.BlockSpec((1, RET_HEADS, RET_DK, RET_DV), lambda b, c: (b, 0, 0, 0))],
        out_shape=[jax.ShapeDtypeStruct((B, T, RET_HEADS * RET_DV), bf16),
                   jax.ShapeDtypeStruct((B, RET_HEADS, RET_DK, RET_DV), f32)],
        scratch_shapes=[pltpu.VMEM((RET_HEADS, RET_DK, RET_DV), f32)],
        compiler_params=_params(("parallel", "arbitrary")),
        name="ret_prompt",
    )(gc, q3, k3, v3, srg3, decay, cross, kdec, W['g_rgn'])


def _row_to_col(row, n):
    eye = lax.broadcasted_iota(jnp.int32, (n, n), 0) == lax.broadcasted_iota(jnp.int32, (n, n), 1)
    return jnp.sum(jnp.where(eye, jnp.broadcast_to(row, (n, n)), 0.0), axis=-1, keepdims=True)


def _ret_sample_kernel(g1_ref, q_ref, k_ref, v_ref, srg_ref, s_ref, g_ref, o_ref, s_out_ref):
    for i in range(q_ref.shape[0]):
        for hh in range(RET_HEADS):
            ks = slice(hh * RET_DK, (hh + 1) * RET_DK)
            vs = slice(hh * RET_DV, (hh + 1) * RET_DV)
            qcol = _row_to_col(q_ref[i, :, ks].astype(f32), RET_DK)
            kcol = _row_to_col(k_ref[i, :, ks].astype(f32), RET_DK)
            v = v_ref[i, :, vs].astype(f32)
            s_new = g1_ref[hh] * s_ref[i, hh] + kcol * v
            s_out_ref[i, hh] = s_new
            o = jnp.sum(qcol * s_new, axis=0, keepdims=True)
            o_ref[i, :, vs] = _group_norm_gate(o, g_ref[:, vs], srg_ref[i, :, vs].astype(f32))


def _ret_sample(q2, k2, v2, srg2, state, l, W, g1):
    Bs = q2.shape[0]
    nb = next(c for c in (4, 2, 1) if Bs % c == 0)
    r3 = lambda a: a.reshape(Bs, 1, a.shape[-1])
    row = lambda n: pl.BlockSpec((nb, 1, n), lambda b: (b, 0, 0))
    sblk = (nb, RET_HEADS, RET_DK, RET_DV)
    o, s_new = pl.pallas_call(
        _ret_sample_kernel,
        grid=(Bs // nb,),
        in_specs=[pl.BlockSpec(memory_space=pltpu.SMEM), row(RET_HEADS * RET_DK), row(RET_HEADS * RET_DK),
                  row(RET_HEADS * RET_DV), row(RET_HEADS * RET_DV),
                  pl.BlockSpec((None,) + sblk, lambda b: (l, b, 0, 0, 0)),
                  pl.BlockSpec((None, 1, RET_HEADS * RET_DV), lambda b: (l, 0, 0))],
        out_specs=[row(RET_HEADS * RET_DV), pl.BlockSpec(sblk, lambda b: (b, 0, 0, 0))],
        out_shape=[jax.ShapeDtypeStruct((Bs, 1, RET_HEADS * RET_DV), bf16),
                   jax.ShapeDtypeStruct((Bs, RET_HEADS, RET_DK, RET_DV), f32)],
        compiler_params=_params(("parallel",)),
        name="ret_sample",
    )(g1, r3(q2), r3(k2), r3(v2), r3(srg2), state, W['g_rgn'])
    return o.reshape(Bs, -1), s_new


def _att_prompt_kernel(qi_ref, ki_ref, q_ref, k_ref, v_ref, o_ref, m_ref, l_ref, acc_ref):
    s_idx = pl.program_id(1)
    qi, ki = qi_ref[s_idx], ki_ref[s_idx]
    tq, tk = q_ref.shape[1], k_ref.shape[1]
    n_lt = tk // LANE

    @pl.when(ki == 0)
    def _():
        m_ref[...] = jnp.full_like(m_ref, MASK_VALUE)
        l_ref[...] = jnp.zeros_like(l_ref)
        acc_ref[...] = jnp.zeros_like(acc_ref)

    def step(diagonal):
        for hd in range(MLA_HEADS):
            v = v_ref[0, :, (hd // 2) * LANE:(hd // 2 + 1) * LANE]
            sl = slice(hd * LANE, (hd + 1) * LANE)
            s = lax.dot_general(q_ref[0, :, sl], k_ref[0, :, sl], _NT, preferred_element_type=f32)
            if diagonal:
                row = lax.broadcasted_iota(jnp.int32, (tq, tk), 0)
                col = lax.broadcasted_iota(jnp.int32, (tq, tk), 1)
                s = jnp.where(col <= row, s, MASK_VALUE)
            m_prev = m_ref[hd]
            smax = s[:, 0:LANE]
            for c in range(1, n_lt):
                smax = jnp.maximum(smax, s[:, c * LANE:(c + 1) * LANE])
            m_new = jnp.maximum(m_prev, jnp.max(smax, axis=-1, keepdims=True))
            a = jnp.exp2(m_prev - m_new)
            p = jnp.exp2(s - jnp.concatenate([m_new] * n_lt, axis=-1))
            psum = p[:, 0:LANE]
            for c in range(1, n_lt):
                psum = psum + p[:, c * LANE:(c + 1) * LANE]
            l_ref[hd] = a * l_ref[hd] + psum
            acc_ref[hd] = a * acc_ref[hd] + jnp.dot(p.astype(bf16), v, preferred_element_type=f32)
            m_ref[hd] = m_new

    @pl.when(ki < qi)
    def _():
        step(False)

    @pl.when(ki == qi)
    def _():
        step(True)
        lane = lax.broadcasted_iota(jnp.int32, (tq, LANE), 1)
        for g in range(MLA_HEADS // 2):
            l0 = jnp.sum(l_ref[2 * g], axis=-1, keepdims=True)
            l1 = jnp.sum(l_ref[2 * g + 1], axis=-1, keepdims=True)
            o = jnp.where(lane < MLA_V, acc_ref[2 * g] / l0, acc_ref[2 * g + 1] / l1)
            o_ref[0, :, g * LANE:(g + 1) * LANE] = o.astype(bf16)


def _att_prompt(qm3, km3, vm3, tq):
    B, T, _ = qm3.shape
    nq = T // tq
    H = MLA_HEADS
    pairs = [(qi, ki) for qi in range(nq) for ki in range(qi + 1)]
    qi_tab = jnp.asarray([p[0] for p in pairs], jnp.int32)
    ki_tab = jnp.asarray([p[1] for p in pairs], jnp.int32)
    grid_spec = pltpu.PrefetchScalarGridSpec(
        num_scalar_prefetch=2,
        grid=(B, len(pairs)),
        in_specs=[pl.BlockSpec((1, tq, H * LANE), lambda b, s, qi, ki: (b, qi[s], 0)),
                  pl.BlockSpec((1, tq, H * LANE), lambda b, s, qi, ki: (b, ki[s], 0)),
                  pl.BlockSpec((1, tq, H * MLA_V), lambda b, s, qi, ki: (b, ki[s], 0))],
        out_specs=pl.BlockSpec((1, tq, H * MLA_V), lambda b, s, qi, ki: (b, qi[s], 0)),
        scratch_shapes=[pltpu.VMEM((H, tq, LANE), f32), pltpu.VMEM((H, tq, LANE), f32),
                        pltpu.VMEM((H, tq, LANE), f32)],
    )
    return pl.pallas_call(
        _att_prompt_kernel,
        grid_spec=grid_spec,
        out_shape=jax.ShapeDtypeStruct((B, T, H * MLA_V), bf16),
        compiler_params=_params(("parallel", "arbitrary")),
        name="att_prompt",
    )(qi_tab, ki_tab, qm3, km3, vm3)


def _hi_lo(x):
    hi = x.astype(bf16).astype(f32)
    return jnp.concatenate([hi, x - hi], axis=0).astype(bf16)


def _rows_select(rows, n=None):
    n, w = n or len(rows), rows[0].shape[-1]
    sub = lax.broadcasted_iota(jnp.int32, (n, w), 0)
    out = jnp.zeros((n, w), rows[0].dtype)
    for i, r in enumerate(rows):
        out = jnp.where(sub == i, jnp.broadcast_to(r, (n, w)), out)
    return out


def _att_sample_kernel(pt_ref, ckv_hbm, kpe_hbm, qm_ref, kmn_ref, ckvn_ref, gkn_ref, wukt_ref, wuktp_ref, wuv_ref,
                       o_ref, ckv_buf, kpe_buf, sem_c, sem_k, aq_ref, cbb_ref, *, l, pp, nj):
    b = pl.program_id(0)
    nb = pl.num_programs(0)
    H = MLA_HEADS
    page = ckv_hbm.shape[2]

    def copies(bb, j, slot):
        out = []
        for i in range(pp):
            pg = pt_ref[bb, j * pp + i]
            out.append(pltpu.make_async_copy(ckv_hbm.at[l, pg], ckv_buf.at[slot, pl.ds(i * page, page)],
                                             sem_c.at[slot]))
            out.append(pltpu.make_async_copy(kpe_hbm.at[l, pg], kpe_buf.at[slot, :, pl.ds(i * page, page)],
                                             sem_k.at[slot]))
        return out

    @pl.when(b == 0)
    def _():
        for c in copies(0, 0, 0) + copies(0, 1, 1):
            c.start()

    qg = qm_ref[0].astype(f32) * jnp.concatenate([gkn_ref[...]] * H, axis=-1)
    sub = lax.broadcasted_iota(jnp.int32, (2 * H, H * LANE), 0)
    head = lax.broadcasted_iota(jnp.int32, (2 * H, H * LANE), 1) // LANE
    qbd = jnp.where(sub == head, jnp.broadcast_to(qg, (2 * H, H * LANE)), 0.0).astype(bf16)
    qabs = jnp.dot(qbd, wuktp_ref[...], preferred_element_type=f32)[0:H]
    aq_ref[0:H * MLA_NOPE, :] = wukt_ref[...]
    aq_ref[H * MLA_NOPE:, :] = _hi_lo(qabs)
    qr = _rows_select([qg[:, h * LANE + MLA_NOPE:h * LANE + MLA_QK] for h in range(H)], 2 * H).astype(bf16)

    def scores(j):
        slot = j % 2
        for c in copies(b, j, slot):
            c.wait()
        cbb_ref[slot] = ckv_buf[slot].astype(bf16)
        cb = cbb_ref[slot]
        kt = kpe_buf[slot]
        r = lax.dot_general(aq_ref[...], cb, _NT, preferred_element_type=f32)
        ss = jnp.concatenate(
            [jnp.sum(jnp.square(r[h * MLA_NOPE:(h + 1) * MLA_NOPE]), axis=0, keepdims=True) for h in range(H)],
            axis=0)
        ss_rope = jnp.sum(kt * kt, axis=0, keepdims=True)
        s_rope = jnp.dot(qr, kt.astype(bf16), preferred_element_type=f32)[0:H]
        s_nope = r[H * MLA_NOPE:H * MLA_NOPE + H] + r[H * MLA_NOPE + H:]
        s = (s_nope + s_rope) * lax.rsqrt((ss + ss_rope) / MLA_QK + EPS)
        if j + 2 < nj:
            for c in copies(b, j + 2, slot):
                c.start()
        else:
            @pl.when(b + 1 < nb)
            def _():
                for c in copies(b + 1, j + 2 - nj, slot):
                    c.start()
        return s

    def accumulate(j, s, m, lsum, ctx):
        m_new = jnp.maximum(m, jnp.max(s, axis=-1, keepdims=True))
        a = jnp.exp2(m - m_new)
        p = jnp.exp2(s - m_new)
        lsum = a * lsum + jnp.sum(p, axis=-1, keepdims=True)
        p16 = jnp.concatenate([p, jnp.zeros_like(p)], axis=0).astype(bf16)
        ctx = a * ctx + jnp.dot(p16, cbb_ref[j % 2], preferred_element_type=f32)[0:H]
        return m_new, lsum, ctx

    m = jnp.full((H, 1), MASK_VALUE, f32)
    lsum = jnp.zeros((H, 1), f32)
    ctx = jnp.zeros((H, ckvn_ref.shape[-1]), f32)
    s_prev = scores(0)
    for j in range(1, nj):
        s_cur = scores(j)
        m, lsum, ctx = accumulate(j - 1, s_prev, m, lsum, ctx)
        s_prev = s_cur
    m, lsum, ctx = accumulate(nj - 1, s_prev, m, lsum, ctx)

    prod = qm_ref[0].astype(f32) * kmn_ref[0].astype(f32)
    s_new = jnp.sum(_rows_select([prod[:, h * LANE:(h + 1) * LANE] for h in range(H)]), axis=-1, keepdims=True)
    m_fin = jnp.maximum(m, s_new)
    a = jnp.exp2(m - m_fin)
    pn = jnp.exp2(s_new - m_fin)
    l_fin = a * lsum + pn
    c_new = ckvn_ref[0].astype(bf16).astype(f32)
    ctx = (a * ctx + pn.astype(bf16).astype(f32) * c_new) / l_fin
    both = jnp.dot(_hi_lo(ctx), wuv_ref[...], preferred_element_type=f32)
    full = both[0:H] + both[H:]
    sub = lax.broadcasted_iota(jnp.int32, full.shape, 0)
    head = lax.broadcasted_iota(jnp.int32, full.shape, 1) // MLA_V
    o_ref[0] = jnp.sum(jnp.where(sub == head, full, 0.0), axis=0, keepdims=True).astype(bf16)


def _att_sample(qm2, kmn2, ckvn2, cache_ckv, cache_kpe_t, page_table, l, W, pp):
    Bs = qm2.shape[0]
    n_pages = page_table.shape[1]
    page, n_lat = cache_ckv.shape[2], cache_ckv.shape[3]
    rope = cache_kpe_t.shape[2]
    H = MLA_HEADS
    nj = n_pages // pp
    assert nj % 2 == 0, "the two staging slots alternate by step parity across requests"
    r3 = lambda a: a.reshape(Bs, 1, a.shape[-1])
    row = lambda n: pl.BlockSpec((1, 1, n), lambda b, pt: (b, 0, 0))
    wl = lambda a: _const_spec((None,) + a.shape[1:], lambda b, pt: (l,) + (0,) * (a.ndim - 1))
    weights = [W['g_kn_p'], W['w_ukt_b'], W['w_ukt_p'], W['w_uv_b']]
    grid_spec = pltpu.PrefetchScalarGridSpec(
        num_scalar_prefetch=1,
        grid=(Bs,),
        in_specs=([pl.BlockSpec(memory_space=pl.ANY), pl.BlockSpec(memory_space=pl.ANY)]
                  + [row(H * LANE), row(H * LANE), row(n_lat)] + [wl(a) for a in weights]),
        out_specs=row(H * MLA_V),
        scratch_shapes=[pltpu.VMEM((2, pp * page, n_lat), f32), pltpu.VMEM((2, rope, pp * page), f32),
                        pltpu.SemaphoreType.DMA((2,)), pltpu.SemaphoreType.DMA((2,)),
                        pltpu.VMEM((H * MLA_NOPE + 2 * H, n_lat), bf16), pltpu.VMEM((2, pp * page, n_lat), bf16)],
    )
    o = pl.pallas_call(
        functools.partial(_att_sample_kernel, l=l, pp=pp, nj=nj),
        grid_spec=grid_spec,
        out_shape=jax.ShapeDtypeStruct((Bs, 1, H * MLA_V), bf16),
        compiler_params=_params(("arbitrary",)),
        name="att_sample",
    )(page_table, cache_ckv, cache_kpe_t, r3(qm2), r3(kmn2), r3(ckvn2), *weights)
    return o.reshape(Bs, -1)


def _out_kernel(x_ref, ca_ref, ra_ref, om_ref, gates_ref, p_ref, wco_ref, wro_ref, wmo_ref, wout_ref, gffn_ref,
                wg_ref, wu_ref, wd_ref, gple_ref, wpg_ref, wpp_ref, o_ref):
    D = x_ref.shape[-1]
    dot = lambda a, b: jnp.dot(a, b, preferred_element_type=f32)
    gate = lambda i: gates_ref[:, i * D:(i + 1) * D].astype(f32)
    mix = (gate(0) * dot(ca_ref[...], wco_ref[...]) + gate(1) * dot(ra_ref[...], wro_ref[...])
           + gate(2) * dot(om_ref[...], wmo_ref[...]))
    x = x_ref[...] + dot(mix.astype(bf16), wout_ref[...])
    hf = _rms(x, gffn_ref[...]).astype(bf16)
    ff = jax.nn.silu(dot(hf, wg_ref[...])) * dot(hf, wu_ref[...])
    x = x + dot(ff.astype(bf16), wd_ref[...])
    hp = _rms(x, gple_ref[...]).astype(bf16)
    pg = jax.nn.sigmoid(dot(hp, wpg_ref[...]))
    o_ref[...] = x + pg * dot(p_ref[...].astype(bf16), wpp_ref[...])


def _out_stage(x2d, ca, ra, om, gates, p3, l, W, tm):
    M, D = x2d.shape
    rows = lambda n: pl.BlockSpec((tm, n), lambda i: (i, 0))
    wl = lambda a: _const_spec((None,) + a.shape[1:], lambda i: (l,) + (0,) * (a.ndim - 1))
    weights = [W['w_conv_out_b'], W['w_ret_out_b'], W['w_mla_out_b'], W['w_out_b'], W['g_ffn'], W['w_gate_b'],
               W['w_up_b'], W['w_down_b'], W['g_ple'], W['w_ple_gate_b'], W['w_ple_proj_b']]
    return pl.pallas_call(
        _out_kernel,
        grid=(M // tm,),
        in_specs=[rows(D), rows(ca.shape[1]), rows(ra.shape[1]), rows(om.shape[1]), rows(gates.shape[1]),
                  pl.BlockSpec((None, tm, p3.shape[-1]), lambda i: (l, i, 0))] + [wl(a) for a in weights],
        out_specs=rows(D),
        out_shape=jax.ShapeDtypeStruct((M, D), f32),
        compiler_params=_params(("parallel",)),
        name="out_stage",
    )(x2d, ca, ra, om, gates, p3, *weights)


def _rope_tables(pos):
    n = pos.shape[0]
    posf = pos.astype(f32)[:, None]

    def cs(half):
        inv = ROPE_BASE ** (-jnp.arange(half, dtype=f32) / half)
        ang = posf * inv[None, :]
        return jnp.cos(ang), jnp.sin(ang)

    c, s = cs(RET_DK // 2)
    c2r = jnp.concatenate([c, c], axis=-1)
    s2r = jnp.concatenate([-s, s], axis=-1)
    c, s = cs(MLA_ROPE // 2)
    hr = MLA_ROPE // 2
    z = lambda w: jnp.zeros((n, w), f32)
    cm = jnp.concatenate([jnp.ones((n, MLA_NOPE), f32), c, c, z(LANE - MLA_QK)], axis=-1)
    s1m = jnp.concatenate([z(MLA_NOPE + hr), s, z(LANE - MLA_QK)], axis=-1)
    s2m = jnp.concatenate([z(MLA_NOPE), -s, z(LANE - MLA_NOPE - hr)], axis=-1)
    return c2r, s2r, cm, s1m, s2m


def _retention_tables(C):
    log_gamma = jnp.log(1.0 - 2.0 ** (-5.0 - jnp.arange(RET_HEADS, dtype=f32)))
    idx = jnp.arange(C, dtype=f32)
    diff = idx[:, None] - idx[None, :]
    decay = jnp.where(diff >= 0, jnp.exp(jnp.maximum(diff, 0.0)[None] * log_gamma[:, None, None]), 0.0)
    cross = jnp.exp((idx + 1.0)[:, None] * log_gamma[None, :])
    kdec = jnp.exp((C - 1.0 - idx)[:, None] * log_gamma[None, :])
    gc = jnp.exp(C * log_gamma)
    lanes = lambda a: jnp.broadcast_to(a.T[:, :, None], (RET_HEADS, C, LANE))
    return gc, decay, lanes(cross), lanes(kdec)


def _pad_heads(w, heads, width):
    w = w.reshape(w.shape[:-1] + (heads, width))
    w = jnp.pad(w, [(0, 0)] * (w.ndim - 1) + [(0, LANE - width)])
    return w.reshape(w.shape[:-2] + (heads * LANE,))


def kernel(x_prompt, x_sample, p_prompt, p_sample, cache_ckv, cache_kpe, state_ret, state_conv, page_table, g_mix, w_in, w_dw, b_dw, g_cln, b_cln, w_conv_out, g_rgn, w_ret_out, g_qa, w_uq, g_qn, g_kva, w_uk, w_uv, g_kn, w_mla_out, w_out, g_ffn, w_gate, w_up, w_down, g_ple, w_ple_gate, w_ple_proj):
    B, T, D = x_prompt.shape
    Bs, Ts, _ = x_sample.shape
    assert Ts == 1, "the sample group decodes one token per request"
    depth = w_in.shape[0]
    conv_dim, width = w_dw.shape[2], w_dw.shape[1]
    q_lora, kv_lora = g_qa.shape[1], g_kva.shape[1]
    page = cache_ckv.shape[2]
    past_len = page_table.shape[1] * page
    assert T % RET_CHUNK == 0 and width - 1 <= HALO_ROWS <= T

    sizes = (2 * conv_dim, RET_HEADS * RET_DK, RET_HEADS * RET_DK, RET_HEADS * RET_DV, RET_HEADS * RET_DV,
             q_lora, kv_lora, MLA_ROPE, 3 * D)
    src = np.concatenate([[0], np.cumsum(sizes)]).tolist()
    offs = tuple(src[:6])
    kr_slab = jnp.pad(w_in[..., src[7]:src[8]], [(0, 0), (0, 0), (MLA_NOPE, LANE - MLA_QK)])

    row = lambda g: g[:, None, :]
    lane_pad = lambda g: jnp.pad(g, [(0, 0), (0, LANE - g.shape[-1])])[:, None, :]
    W = dict(
        g_mix=row(g_mix), g_qa=row(g_qa), g_kva=row(g_kva),
        w_in_a=w_in[..., :src[5]].astype(bf16),
        w_in_qk=jnp.concatenate([w_in[..., src[5]:src[6]], kr_slab], axis=-1).astype(bf16),
        w_in_kva=w_in[..., src[6]:src[7]].astype(bf16),
        w_in_g=w_in[..., src[8]:].astype(bf16),
        w_uq_p=_pad_heads(w_uq, MLA_HEADS, MLA_QK).astype(bf16),
        g_qn_p=lane_pad(g_qn * (MLA_QK ** -0.5 * LOG2_E)),
        w_uk_p=_pad_heads(w_uk, MLA_HEADS, MLA_NOPE).astype(bf16),
        g_kn_p=lane_pad(g_kn),
        w_uv_b=w_uv.astype(bf16),
        w_ukt_b=jnp.swapaxes(w_uk, 1, 2).astype(bf16),
        w_ukt_p=jnp.swapaxes(_pad_heads(w_uk, MLA_HEADS, MLA_NOPE), 1, 2).astype(bf16),
        w_dw=w_dw, b_dw=row(b_dw), g_cln=row(g_cln), b_cln=row(b_cln), g_rgn=row(g_rgn),
        w_conv_out_b=w_conv_out.astype(bf16), w_ret_out_b=w_ret_out.astype(bf16),
        w_mla_out_b=w_mla_out.astype(bf16), w_out_b=w_out.astype(bf16), g_ffn=row(g_ffn),
        w_gate_b=w_gate.astype(bf16), w_up_b=w_up.astype(bf16), w_down_b=w_down.astype(bf16),
        g_ple=row(g_ple), w_ple_gate_b=w_ple_gate.astype(bf16), w_ple_proj_b=w_ple_proj.astype(bf16),
    )

    tabs_p = _rope_tables(jnp.arange(T, dtype=jnp.int32))
    tabs_s = _rope_tables(jnp.full((Bs,), past_len, jnp.int32))
    ret_tabs = _retention_tables(RET_CHUNK)
    g1 = _retention_tables(1)[0]

    tm = min(ROW_TILE, T)
    tq = min(ATT_TILE, T)
    tr = min(RET_TILE, T)
    n_pages = page_table.shape[1]
    pp = next(c for c in (PAGES_PER_STEP, 8, 4, 2, 1) if n_pages % (2 * c) == 0)

    cache_kpe_t = jnp.swapaxes(cache_kpe, 2, 3)

    xp = x_prompt.reshape(B * T, D)
    xs = x_sample.reshape(Bs, D)
    pp3 = p_prompt.reshape(depth, B * T, -1)
    ps3 = p_sample.reshape(depth, Bs, -1)
    outs = [[] for _ in range(8)]
    for l in range(depth):
        u, q, k, v, srg, qm, km, vm, ckv, kpe, gates, ca = _in_proj(xp, tabs_p, l, W, offs, tm, T // tm, True)
        s3 = lambda a: a.reshape(B, T, a.shape[-1])
        u3 = s3(u)
        ra, s_new = _ret_prompt(s3(q), s3(k), s3(v), s3(srg), l, W, ret_tabs, tr)
        om = _att_prompt(s3(qm), s3(km), s3(vm), tq)
        xp = _out_stage(xp, ca, ra.reshape(B * T, -1), om.reshape(B * T, -1), gates, pp3, l, W, tm)
        outs[0].append(s3(ckv)); outs[1].append(s3(kpe)); outs[4].append(s_new)
        outs[6].append(u3[:, T - (width - 1):])

        u, q, k, v, srg, qm, km, vm, ckv, kpe, gates = _in_proj(xs, tabs_s, l, W, offs, Bs, 1, False)
        ca = _conv_sample(state_conv, u, l, W)
        ra, s_new = _ret_sample(q, k, v, srg, state_ret, l, W, g1)
        om = _att_sample(qm, km, ckv, cache_ckv, cache_kpe_t, page_table, l, W, pp)
        xs = _out_stage(xs, ca, ra, om, gates, ps3, l, W, Bs)
        outs[2].append(ckv[:, None]); outs[3].append(kpe[:, None]); outs[5].append(s_new)
        outs[7].append(jnp.concatenate([state_conv[l][:, 1:], u[:, None]], axis=1))

    st = [jnp.stack(o) for o in outs]
    return (xp.reshape(B, T, D), xs.reshape(Bs, Ts, D), st[0], st[1], st[2], st[3], st[4], st[5], st[6], st[7])
```

```python
import functools

import jax
import jax.numpy as jnp
import numpy as np
from jax import lax
from jax.experimental import pallas as pl
from jax.experimental.pallas import tpu as pltpu

f32 = jnp.float32
bf16 = jnp.bfloat16

EPS = 1e-6
ROPE_BASE = 10000.0
RET_HEADS = 4
RET_DK = 128
RET_DV = 256
RET_CHUNK = 128
MLA_HEADS = 8
MLA_NOPE = 64
MLA_ROPE = 32
MLA_V = 64
MLA_QK = MLA_NOPE + MLA_ROPE
LANE = 128
SUBLANES = 8
HALO_ROWS = 32
VMEM_LIMIT = 56 * 1024 * 1024
MASK_VALUE = -1e30
LOG2_E = 1.4426950408889634
ROW_TILE = 256
ATT_TILE = 512
RET_TILE = 1024
PAGES_PER_STEP = 32
REQUESTS_PER_STEP = 2

_NT = (((1,), (1,)), ((), ()))
_TN = (((0,), (0,)), ((), ()))


def _rms(x, g):
    return x * lax.rsqrt(jnp.mean(x * x, axis=-1, keepdims=True) + EPS) * g


def _const_spec(shape, index):
    return pl.BlockSpec(shape, index, pipeline_mode=pl.Buffered(1))


def _params(sem):
    return pltpu.CompilerParams(dimension_semantics=sem, vmem_limit_bytes=VMEM_LIMIT)


def _mla_rope(x, cm, s1, s2):
    return x * cm + pltpu.roll(x, MLA_ROPE // 2, 1) * s1 + pltpu.roll(x, LANE - MLA_ROPE // 2, 1) * s2


def _in_proj_kernel(*refs, offs, conv_blocks):
    (x_ref, c2r_ref, s2r_ref, cm_ref, s1m_ref, s2m_ref, gmix_ref, win_ref, wqk_ref, wkva_ref, wg_ref, gqa_ref,
     wuq_ref, gqn_ref, gkva_ref, wuk_ref, wuv_ref, gkn_ref) = refs[:18]
    refs = refs[18:]
    if conv_blocks is not None:
        wdw_ref, bdw_ref, gcln_ref, bcln_ref = refs[:4]
        refs = refs[4:]
    u_ref, q_ref, k_ref, v_ref, srg_ref, qm_ref, km_ref, vm_ref, ckv_ref, kpe_ref, gates_ref = refs[:11]
    refs = refs[11:]
    o_glu, o_rq, o_rk, o_rv, o_rg, o_qa = offs
    h = _rms(x_ref[...], gmix_ref[...]).astype(bf16)

    def proj(c0, c1):
        return jnp.dot(h, win_ref[:, c0:c1], preferred_element_type=f32)

    def proj_all(w_ref):
        return jnp.dot(h, w_ref[...], preferred_element_type=f32)

    glu = proj(o_glu, o_rq)
    half = (o_rq - o_glu) // 2
    u = glu[:, :half] * jax.nn.sigmoid(glu[:, half:])
    u_ref[...] = u

    if conv_blocks is not None:
        ca_ref, carry, ubuf, sbuf = refs
        i = pl.program_id(0)

        @pl.when(i == 0)
        def _():
            carry[...] = jnp.zeros_like(carry)

        tm = u.shape[0]
        ubuf[0:HALO_ROWS, :] = jnp.where(i % conv_blocks > 0, carry[...], 0.0)
        ubuf[HALO_ROWS:, :] = u
        carry[...] = u[tm - HALO_ROWS:, :]
        for step in _conv_ln_swish_steps(ubuf, sbuf, wdw_ref, bdw_ref, gcln_ref, bcln_ref, ca_ref, tm,
                                         wdw_ref.shape[0], tm // 4):
            step()

    c2, s2 = c2r_ref[...], s2r_ref[...]
    rq = proj(o_rq, o_rk)
    rk = proj(o_rk, o_rv)
    for hh in range(RET_HEADS):
        sl = slice(hh * RET_DK, (hh + 1) * RET_DK)
        xq, xk = rq[:, sl], rk[:, sl]
        q_ref[:, sl] = (xq * c2 + pltpu.roll(xq, RET_DK // 2, 1) * s2).astype(bf16)
        k_ref[:, sl] = ((xk * c2 + pltpu.roll(xk, RET_DK // 2, 1) * s2) * (RET_DK ** -0.5)).astype(bf16)
    v_ref[...] = proj(o_rv, o_rg).astype(bf16)
    srg_ref[...] = jax.nn.silu(proj(o_rg, o_qa)).astype(bf16)

    cm, s1, s2m = cm_ref[...], s1m_ref[...], s2m_ref[...]
    qa_kr = proj_all(wqk_ref)
    q_lora = gqa_ref.shape[-1]
    c_q = _rms(qa_kr[:, :q_lora], gqa_ref[...]).astype(bf16)
    qm = jnp.dot(c_q, wuq_ref[...], preferred_element_type=f32)
    gqn = gqn_ref[...]
    for hh in range(MLA_HEADS):
        sl = slice(hh * LANE, (hh + 1) * LANE)
        r = _mla_rope(qm[:, sl], cm, s1, s2m)
        ss = jnp.sum(r * r, axis=-1, keepdims=True)
        qm_ref[:, sl] = (r * lax.rsqrt(ss / MLA_QK + EPS) * gqn).astype(bf16)

    ckv = _rms(proj_all(wkva_ref), gkva_ref[...])
    ckv_ref[...] = ckv
    cb = ckv.astype(bf16)
    kpe_slab = _mla_rope(qa_kr[:, q_lora:], cm, s1, s2m)
    kpe_ref[...] = kpe_slab[:, MLA_NOPE:MLA_QK]
    kn = jnp.dot(cb, wuk_ref[...], preferred_element_type=f32)
    gkn = gkn_ref[...]
    for hh in range(MLA_HEADS):
        sl = slice(hh * LANE, (hh + 1) * LANE)
        kk = kn[:, sl] + kpe_slab
        ss = jnp.sum(kk * kk, axis=-1, keepdims=True)
        km_ref[:, sl] = (kk * lax.rsqrt(ss / MLA_QK + EPS) * gkn).astype(bf16)
    vm_ref[...] = jnp.dot(cb, wuv_ref[...], preferred_element_type=f32).astype(bf16)

    gates_ref[...] = jax.nn.sigmoid(proj_all(wg_ref)).astype(bf16)

def _in_proj(x2d, tabs, l, W, offs, tm, n_pos_blocks, fuse_conv):
    M, D = x2d.shape
    conv_dim = (offs[1] - offs[0]) // 2
    n_gates = W['w_in_g'].shape[-1]
    kv_lora = W['w_in_kva'].shape[-1]
    rows = lambda n: pl.BlockSpec((tm, n), lambda i: (i, 0))
    tab = pl.BlockSpec((tm, LANE), lambda i: (i % n_pos_blocks, 0))
    wl = lambda a: _const_spec((None,) + a.shape[1:], lambda i: (l,) + (0,) * (a.ndim - 1))
    out_dims = [(conv_dim, f32), (RET_HEADS * RET_DK, bf16), (RET_HEADS * RET_DK, bf16), (RET_HEADS * RET_DV, bf16),
                (RET_HEADS * RET_DV, bf16), (MLA_HEADS * LANE, bf16), (MLA_HEADS * LANE, bf16),
                (MLA_HEADS * MLA_V, bf16), (kv_lora, f32), (MLA_ROPE, f32), (n_gates, bf16)]
    weights = [W['g_mix'], W['w_in_a'], W['w_in_qk'], W['w_in_kva'], W['w_in_g'], W['g_qa'], W['w_uq_p'],
               W['g_qn_p'], W['g_kva'], W['w_uk_p'], W['w_uv_b'],
               W['g_kn_p']]
    scratch = []
    if fuse_conv:
        assert tm >= HALO_ROWS
        weights += [W['w_dw'], W['b_dw'], W['g_cln'], W['b_cln']]
        out_dims.append((conv_dim, bf16))
        scratch = [pltpu.VMEM((HALO_ROWS, conv_dim), f32), pltpu.VMEM((HALO_ROWS + tm, conv_dim), f32),
                   pltpu.VMEM((SUBLANES, HALO_ROWS + tm, conv_dim), f32)]
    return pl.pallas_call(
        functools.partial(_in_proj_kernel, offs=offs, conv_blocks=n_pos_blocks if fuse_conv else None),
        grid=(M // tm,),
        in_specs=[rows(D)] + [tab] * 5 + [wl(a) for a in weights],
        out_specs=[rows(n) for n, _ in out_dims],
        out_shape=[jax.ShapeDtypeStruct((M, n), dt) for n, dt in out_dims],
        scratch_shapes=scratch,
        compiler_params=_params(("arbitrary",) if fuse_conv else ("parallel",)),
        name="in_proj",
    )(x2d, *tabs, *weights)


def _ln_swish(y, g, b):
    mu = jnp.mean(y, axis=-1, keepdims=True)
    d = y - mu
    var = jnp.mean(d * d, axis=-1, keepdims=True)
    z = d * lax.rsqrt(var + EPS) * g + b
    return z * jax.nn.sigmoid(z)


def _conv_ln_swish_steps(ubuf, sbuf, w_ref, b_ref, g_ref, bl_ref, o_ref, tc, width, rc):
    first = HALO_ROWS - (width - 1)
    C = ubuf.shape[-1]
    R = HALO_ROWS + tc

    def shifted_copies():
        sbuf[0] = ubuf[...]
        for ph in range(1, SUBLANES):
            sbuf[ph, 0:R - SUBLANES, :] = ubuf[ph:R - SUBLANES + ph, :]

    def chunk(r0):
        cols = []
        for c0 in range(0, C, LANE):
            acc = jnp.zeros((rc, LANE), f32) + b_ref[:, c0:c0 + LANE]
            for j in range(width):
                ph = (first + j) % SUBLANES
                base = first + j - ph + r0
                acc = acc + w_ref[j:j + 1, c0:c0 + LANE] * sbuf[ph, base:base + rc, c0:c0 + LANE]
            cols.append(acc)
        y = jnp.concatenate(cols, axis=-1)
        o_ref[r0:r0 + rc, :] = _ln_swish(y, g_ref[...], bl_ref[...]).astype(bf16)

    return [shifted_copies] + [functools.partial(chunk, r0) for r0 in range(0, tc, rc)]


def _conv_sample_kernel(st_ref, u_ref, w_ref, b_ref, g_ref, bl_ref, o_ref, *, width):
    y = u_ref[...] * w_ref[width - 1:width, :] + b_ref[...]
    for j in range(width - 1):
        y = y + st_ref[:, j, :] * w_ref[j:j + 1, :]
    o_ref[...] = _ln_swish(y, g_ref[...], bl_ref[...]).astype(bf16)


def _conv_sample(state, u2, l, W):
    Bs, C = u2.shape
    width = W['w_dw'].shape[1]
    wl = lambda a: pl.BlockSpec((None,) + a.shape[1:], lambda i: (l,) + (0,) * (a.ndim - 1))
    weights = [W['w_dw'], W['b_dw'], W['g_cln'], W['b_cln']]
    return pl.pallas_call(
        functools.partial(_conv_sample_kernel, width=width),
        grid=(1,),
        in_specs=[pl.BlockSpec((None,) + state.shape[1:], lambda i: (l, 0, 0, 0)),
                  pl.BlockSpec(u2.shape, lambda i: (0, 0))] + [wl(a) for a in weights],
        out_specs=pl.BlockSpec((Bs, C), lambda i: (0, 0)),
        out_shape=jax.ShapeDtypeStruct((Bs, C), bf16),
        compiler_params=_params(("arbitrary",)),
        name="conv_sample",
    )(state, u2, *weights)


def _group_norm_gate(o, g, gate):
    mu = jnp.mean(o, axis=-1, keepdims=True)
    d = o - mu
    var = jnp.mean(d * d, axis=-1, keepdims=True)
    return (d * lax.rsqrt(var + EPS) * g * gate).astype(bf16)


def _ret_prompt_kernel(gc_ref, q_ref, k_ref, v_ref, srg_ref, decay_ref, cross_ref, kdec_ref, g_ref,
                       o_ref, s_out_ref, s_ref):
    c = pl.program_id(1)

    @pl.when(c == 0)
    def _():
        s_ref[...] = jnp.zeros_like(s_ref)

    C = RET_CHUNK
    for hh in range(RET_HEADS):
        ks = slice(hh * RET_DK, (hh + 1) * RET_DK)
        vs = slice(hh * RET_DV, (hh + 1) * RET_DV)
        cr = jnp.concatenate([cross_ref[hh]] * (RET_DV // LANE), axis=-1)
        s_cur = s_ref[hh]
        for r0 in range(0, q_ref.shape[1], C):
            rows = slice(r0, r0 + C)
            q, k, v = q_ref[0, rows, ks], k_ref[0, rows, ks], v_ref[0, rows, vs]
            scores = lax.dot_general(q, k, _NT, preferred_element_type=f32) * decay_ref[hh]
            o = jnp.dot(scores.astype(bf16), v, preferred_element_type=f32)
            o = o + jnp.dot(q, s_cur.astype(bf16), preferred_element_type=f32) * cr
            kd = (k.astype(f32) * kdec_ref[hh]).astype(bf16)
            s_cur = gc_ref[hh] * s_cur + lax.dot_general(kd, v, _TN, preferred_element_type=f32)
            o_ref[0, rows, vs] = _group_norm_gate(o, g_ref[:, vs], srg_ref[0, rows, vs].astype(f32))
        s_ref[hh] = s_cur

    @pl.when(c == pl.num_programs(1) - 1)
    def _():
        s_out_ref[0] = s_ref[...]


def _ret_prompt(q3, k3, v3, srg3, l, W, tabs, tr):
    B, T, _ = q3.shape
    gc, decay, cross, kdec = tabs
    blk = lambda n: pl.BlockSpec((1, tr, n), lambda b, c: (b, c, 0))
    tab = _const_spec((RET_HEADS, RET_CHUNK, LANE), lambda b, c: (0, 0, 0))
    return pl.pallas_call(
        _ret_prompt_kernel,
        grid=(B, T // tr),
        in_specs=[pl.BlockSpec(memory_space=pltpu.SMEM), blk(RET_HEADS * RET_DK), blk(RET_HEADS * RET_DK),
                  blk(RET_HEADS * RET_DV), blk(RET_HEADS * RET_DV), tab, tab, tab,
                  _const_spec((None, 1, RET_HEADS * RET_DV), lambda b, c: (l, 0, 0))],
        out_specs=[blk(RET_HEADS * RET_DV),
                   pl.BlockSpec((1, RET_HEADS, RET_DK, RET_DV), lambda b, c: (b, 0, 0, 0))],
        out_shape=[jax.ShapeDtypeStruct((B, T, RET_HEADS * RET_DV), bf16),
                   jax.ShapeDtypeStruct((B, RET_HEADS, RET_DK, RET_DV), f32)],
        scratch_shapes=[pltpu.VMEM((RET_HEADS, RET_DK, RET_DV), f32)],
        compiler_params=_params(("parallel", "arbitrary")),
        name="ret_prompt",
    )(gc, q3, k3, v3, srg3, decay, cross, kdec, W['g_rgn'])


def _row_to_col(row, n):
    eye = lax.broadcasted_iota(jnp.int32, (n, n), 0) == lax.broadcasted_iota(jnp.int32, (n, n), 1)
    return jnp.sum(jnp.where(eye, jnp.broadcast_to(row, (n, n)), 0.0), axis=-1, keepdims=True)


def _ret_sample_kernel(g1_ref, q_ref, k_ref, v_ref, srg_ref, s_ref, g_ref, o_ref, s_out_ref):
    for i in range(q_ref.shape[0]):
        for hh in range(RET_HEADS):
            ks = slice(hh * RET_DK, (hh + 1) * RET_DK)
            vs = slice(hh * RET_DV, (hh + 1) * RET_DV)
            qcol = _row_to_col(q_ref[i, :, ks].astype(f32), RET_DK)
            kcol = _row_to_col(k_ref[i, :, ks].astype(f32), RET_DK)
            v = v_ref[i, :, vs].astype(f32)
            s_new = g1_ref[hh] * s_ref[i, hh] + kcol * v
            s_out_ref[i, hh] = s_new
            o = jnp.sum(qcol * s_new, axis=0, keepdims=True)
            o_ref[i, :, vs] = _group_norm_gate(o, g_ref[:, vs], srg_ref[i, :, vs].astype(f32))


def _ret_sample(q2, k2, v2, srg2, state, l, W, g1):
    Bs = q2.shape[0]
    nb = next(c for c in (4, 2, 1) if Bs % c == 0)
    r3 = lambda a: a.reshape(Bs, 1, a.shape[-1])
    row = lambda n: pl.BlockSpec((nb, 1, n), lambda b: (b, 0, 0))
    sblk = (nb, RET_HEADS, RET_DK, RET_DV)
    o, s_new = pl.pallas_call(
        _ret_sample_kernel,
        grid=(Bs // nb,),
        in_specs=[pl.BlockSpec(memory_space=pltpu.SMEM), row(RET_HEADS * RET_DK), row(RET_HEADS * RET_DK),
                  row(RET_HEADS * RET_DV), row(RET_HEADS * RET_DV),
                  pl.BlockSpec((None,) + sblk, lambda b: (l, b, 0, 0, 0)),
                  pl.BlockSpec((None, 1, RET_HEADS * RET_DV), lambda b: (l, 0, 0))],
        out_specs=[row(RET_HEADS * RET_DV), pl.BlockSpec(sblk, lambda b: (b, 0, 0, 0))],
        out_shape=[jax.ShapeDtypeStruct((Bs, 1, RET_HEADS * RET_DV), bf16),
                   jax.ShapeDtypeStruct((Bs, RET_HEADS, RET_DK, RET_DV), f32)],
        compiler_params=_params(("parallel",)),
        name="ret_sample",
    )(g1, r3(q2), r3(k2), r3(v2), r3(srg2), state, W['g_rgn'])
    return o.reshape(Bs, -1), s_new


def _att_prompt_kernel(qi_ref, ki_ref, q_ref, k_ref, v_ref, o_ref, m_ref, l_ref, acc_ref):
    s_idx = pl.program_id(1)
    qi, ki = qi_ref[s_idx], ki_ref[s_idx]
    tq, tk = q_ref.shape[1], k_ref.shape[1]
    n_lt = tk // LANE

    @pl.when(ki == 0)
    def _():
        m_ref[...] = jnp.full_like(m_ref, MASK_VALUE)
        l_ref[...] = jnp.zeros_like(l_ref)
        acc_ref[...] = jnp.zeros_like(acc_ref)

    def step(diagonal):
        for hd in range(MLA_HEADS):
            v = v_ref[0, :, (hd // 2) * LANE:(hd // 2 + 1) * LANE]
            sl = slice(hd * LANE, (hd + 1) * LANE)
            s = lax.dot_general(q_ref[0, :, sl], k_ref[0, :, sl], _NT, preferred_element_type=f32)
            if diagonal:
                row = lax.broadcasted_iota(jnp.int32, (tq, tk), 0)
                col = lax.broadcasted_iota(jnp.int32, (tq, tk), 1)
                s = jnp.where(col <= row, s, MASK_VALUE)
            m_prev = m_ref[hd]
            smax = s[:, 0:LANE]
            for c in range(1, n_lt):
                smax = jnp.maximum(smax, s[:, c * LANE:(c + 1) * LANE])
            m_new = jnp.maximum(m_prev, jnp.max(smax, axis=-1, keepdims=True))
            a = jnp.exp2(m_prev - m_new)
            p = jnp.exp2(s - jnp.concatenate([m_new] * n_lt, axis=-1))
            psum = p[:, 0:LANE]
            for c in range(1, n_lt):
                psum = psum + p[:, c * LANE:(c + 1) * LANE]
            l_ref[hd] = a * l_ref[hd] + psum
            acc_ref[hd] = a * acc_ref[hd] + jnp.dot(p.astype(bf16), v, preferred_element_type=f32)
            m_ref[hd] = m_new

    @pl.when(ki < qi)
    def _():
        step(False)

    @pl.when(ki == qi)
    def _():
        step(True)
        lane = lax.broadcasted_iota(jnp.int32, (tq, LANE), 1)
        for g in range(MLA_HEADS // 2):
            l0 = jnp.sum(l_ref[2 * g], axis=-1, keepdims=True)
            l1 = jnp.sum(l_ref[2 * g + 1], axis=-1, keepdims=True)
            o = jnp.where(lane < MLA_V, acc_ref[2 * g] / l0, acc_ref[2 * g + 1] / l1)
            o_ref[0, :, g * LANE:(g + 1) * LANE] = o.astype(bf16)


def _att_prompt(qm3, km3, vm3, tq):
    B, T, _ = qm3.shape
    nq = T // tq
    H = MLA_HEADS
    pairs = [(qi, ki) for qi in range(nq) for ki in range(qi + 1)]
    qi_tab = jnp.asarray([p[0] for p in pairs], jnp.int32)
    ki_tab = jnp.asarray([p[1] for p in pairs], jnp.int32)
    grid_spec = pltpu.PrefetchScalarGridSpec(
        num_scalar_prefetch=2,
        grid=(B, len(pairs)),
        in_specs=[pl.BlockSpec((1, tq, H * LANE), lambda b, s, qi, ki: (b, qi[s], 0)),
                  pl.BlockSpec((1, tq, H * LANE), lambda b, s, qi, ki: (b, ki[s], 0)),
                  pl.BlockSpec((1, tq, H * MLA_V), lambda b, s, qi, ki: (b, ki[s], 0))],
        out_specs=pl.BlockSpec((1, tq, H * MLA_V), lambda b, s, qi, ki: (b, qi[s], 0)),
        scratch_shapes=[pltpu.VMEM((H, tq, LANE), f32), pltpu.VMEM((H, tq, LANE), f32),
                        pltpu.VMEM((H, tq, LANE), f32)],
    )
    return pl.pallas_call(
        _att_prompt_kernel,
        grid_spec=grid_spec,
        out_shape=jax.ShapeDtypeStruct((B, T, H * MLA_V), bf16),
        compiler_params=_params(("parallel", "arbitrary")),
        name="att_prompt",
    )(qi_tab, ki_tab, qm3, km3, vm3)


def _hi_lo(x):
    hi = x.astype(bf16).astype(f32)
    return jnp.concatenate([hi, x - hi], axis=0).astype(bf16)


def _rows_select(rows, n=None):
    n, w = n or len(rows), rows[0].shape[-1]
    sub = lax.broadcasted_iota(jnp.int32, (n, w), 0)
    out = jnp.zeros((n, w), rows[0].dtype)
    for i, r in enumerate(rows):
        out = jnp.where(sub == i, jnp.broadcast_to(r, (n, w)), out)
    return out


def _att_sample_kernel(pt_ref, ckv_hbm, kpe_hbm, qm_ref, kmn_ref, ckvn_ref, gkn_ref, wukt_ref, wuktp_ref, wuv_ref,
                       o_ref, ckv_buf, kpe_buf, sem_c, sem_k, aq_ref, cbb_ref, *, l, pp, nj, nr):
    b = pl.program_id(0)
    nb = pl.num_programs(0)
    H = MLA_HEADS
    page = ckv_hbm.shape[2]

    def copies(bb, r, j, slot):
        out = []
        for i in range(pp):
            pg = pt_ref[bb * nr + r, j * pp + i]
            out.append(pltpu.make_async_copy(ckv_hbm.at[l, pg], ckv_buf.at[r, slot, pl.ds(i * page, page)],
                                             sem_c.at[r, slot]))
            out.append(pltpu.make_async_copy(kpe_hbm.at[l, pg], kpe_buf.at[r, slot, :, pl.ds(i * page, page)],
                                             sem_k.at[r, slot]))
        return out

    @pl.when(b == 0)
    def _():
        for r in range(nr):
            for c in copies(0, r, 0, 0) + copies(0, r, 1, 1):
                c.start()

    gkn = jnp.concatenate([gkn_ref[...]] * H, axis=-1)
    sub = lax.broadcasted_iota(jnp.int32, (2 * H, H * LANE), 0)
    head = lax.broadcasted_iota(jnp.int32, (2 * H, H * LANE), 1) // LANE
    qr = []
    for r in range(nr):
        qg = qm_ref[r].astype(f32) * gkn
        qbd = jnp.where(sub == head, jnp.broadcast_to(qg, (2 * H, H * LANE)), 0.0).astype(bf16)
        qabs = jnp.dot(qbd, wuktp_ref[...], preferred_element_type=f32)[0:H]
        aq_ref[r, 0:H * MLA_NOPE, :] = wukt_ref[...]
        aq_ref[r, H * MLA_NOPE:, :] = _hi_lo(qabs)
        qr.append(_rows_select([qg[:, h * LANE + MLA_NOPE:h * LANE + MLA_QK] for h in range(H)],
                               2 * H).astype(bf16))

    def scores(r, j):
        slot = j % 2
        cbb_ref[r, slot] = ckv_buf[r, slot].astype(bf16)
        cb = cbb_ref[r, slot]
        kt = kpe_buf[r, slot]
        x = lax.dot_general(aq_ref[r], cb, _NT, preferred_element_type=f32)
        ss = jnp.concatenate(
            [jnp.sum(jnp.square(x[h * MLA_NOPE:(h + 1) * MLA_NOPE]), axis=0, keepdims=True) for h in range(H)],
            axis=0)
        ss_rope = jnp.sum(kt * kt, axis=0, keepdims=True)
        s_rope = jnp.dot(qr[r], kt.astype(bf16), preferred_element_type=f32)[0:H]
        s_nope = x[H * MLA_NOPE:H * MLA_NOPE + H] + x[H * MLA_NOPE + H:]
        s = (s_nope + s_rope) * lax.rsqrt((ss + ss_rope) / MLA_QK + EPS)
        if j + 2 < nj:
            for c in copies(b, r, j + 2, slot):
                c.start()
        else:
            @pl.when(b + 1 < nb)
            def _():
                for c in copies(b + 1, r, j + 2 - nj, slot):
                    c.start()
        return s

    def step_scores(j):
        for r in range(nr):
            for c in copies(b, r, j, j % 2):
                c.wait()
        return [scores(r, j) for r in range(nr)]

    def accumulate(r, j, s, m, lsum, ctx):
        m_new = jnp.maximum(m, jnp.max(s, axis=-1, keepdims=True))
        a = jnp.exp2(m - m_new)
        p = jnp.exp2(s - m_new)
        lsum = a * lsum + jnp.sum(p, axis=-1, keepdims=True)
        p16 = jnp.concatenate([p, jnp.zeros_like(p)], axis=0).astype(bf16)
        ctx = a * ctx + jnp.dot(p16, cbb_ref[r, j % 2], preferred_element_type=f32)[0:H]
        return m_new, lsum, ctx

    state = [(jnp.full((H, 1), MASK_VALUE, f32), jnp.zeros((H, 1), f32), jnp.zeros((H, ckvn_ref.shape[-1]), f32))
             for _ in range(nr)]
    s_prev = step_scores(0)
    for j in range(1, nj):
        s_cur = step_scores(j)
        state = [accumulate(r, j - 1, s_prev[r], *state[r]) for r in range(nr)]
        s_prev = s_cur
    state = [accumulate(r, nj - 1, s_prev[r], *state[r]) for r in range(nr)]

    for r in range(nr):
        m, lsum, ctx = state[r]
        prod = qm_ref[r].astype(f32) * kmn_ref[r].astype(f32)
        s_new = jnp.sum(_rows_select([prod[:, h * LANE:(h + 1) * LANE] for h in range(H)]), axis=-1,
                        keepdims=True)
        m_fin = jnp.maximum(m, s_new)
        a = jnp.exp2(m - m_fin)
        pn = jnp.exp2(s_new - m_fin)
        l_fin = a * lsum + pn
        c_new = ckvn_ref[r].astype(bf16).astype(f32)
        ctx = (a * ctx + pn.astype(bf16).astype(f32) * c_new) / l_fin
        both = jnp.dot(_hi_lo(ctx), wuv_ref[...], preferred_element_type=f32)
        full = both[0:H] + both[H:]
        sub_o = lax.broadcasted_iota(jnp.int32, full.shape, 0)
        head_o = lax.broadcasted_iota(jnp.int32, full.shape, 1) // MLA_V
        o_ref[r] = jnp.sum(jnp.where(sub_o == head_o, full, 0.0), axis=0, keepdims=True).astype(bf16)


def _att_sample(qm2, kmn2, ckvn2, cache_ckv, cache_kpe_t, page_table, l, W, pp, nr):
    Bs = qm2.shape[0]
    n_pages = page_table.shape[1]
    page, n_lat = cache_ckv.shape[2], cache_ckv.shape[3]
    rope = cache_kpe_t.shape[2]
    H = MLA_HEADS
    nj = n_pages // pp
    assert nj % 2 == 0, "the two staging slots alternate by step parity across grid steps"
    assert Bs % nr == 0
    r3 = lambda a: a.reshape(Bs, 1, a.shape[-1])
    row = lambda n: pl.BlockSpec((nr, 1, n), lambda b, pt: (b, 0, 0))
    wl = lambda a: _const_spec((None,) + a.shape[1:], lambda b, pt: (l,) + (0,) * (a.ndim - 1))
    weights = [W['g_kn_p'], W['w_ukt_b'], W['w_ukt_p'], W['w_uv_b']]
    grid_spec = pltpu.PrefetchScalarGridSpec(
        num_scalar_prefetch=1,
        grid=(Bs // nr,),
        in_specs=([pl.BlockSpec(memory_space=pl.ANY), pl.BlockSpec(memory_space=pl.ANY)]
                  + [row(H * LANE), row(H * LANE), row(n_lat)] + [wl(a) for a in weights]),
        out_specs=row(H * MLA_V),
        scratch_shapes=[pltpu.VMEM((nr, 2, pp * page, n_lat), f32), pltpu.VMEM((nr, 2, rope, pp * page), f32),
                        pltpu.SemaphoreType.DMA((nr, 2)), pltpu.SemaphoreType.DMA((nr, 2)),
                        pltpu.VMEM((nr, H * MLA_NOPE + 2 * H, n_lat), bf16),
                        pltpu.VMEM((nr, 2, pp * page, n_lat), bf16)],
    )
    o = pl.pallas_call(
        functools.partial(_att_sample_kernel, l=l, pp=pp, nj=nj, nr=nr),
        grid_spec=grid_spec,
        out_shape=jax.ShapeDtypeStruct((Bs, 1, H * MLA_V), bf16),
        compiler_params=_params(("arbitrary",)),
        name="att_sample",
    )(page_table, cache_ckv, cache_kpe_t, r3(qm2), r3(kmn2), r3(ckvn2), *weights)
    return o.reshape(Bs, -1)


def _out_kernel(x_ref, ca_ref, ra_ref, om_ref, gates_ref, p_ref, wco_ref, wro_ref, wmo_ref, wout_ref, gffn_ref,
                wg_ref, wu_ref, wd_ref, gple_ref, wpg_ref, wpp_ref, o_ref):
    D = x_ref.shape[-1]
    dot = lambda a, b: jnp.dot(a, b, preferred_element_type=f32)
    gate = lambda i: gates_ref[:, i * D:(i + 1) * D].astype(f32)
    mix = (gate(0) * dot(ca_ref[...], wco_ref[...]) + gate(1) * dot(ra_ref[...], wro_ref[...])
           + gate(2) * dot(om_ref[...], wmo_ref[...]))
    x = x_ref[...] + dot(mix.astype(bf16), wout_ref[...])
    hf = _rms(x, gffn_ref[...]).astype(bf16)
    ff = jax.nn.silu(dot(hf, wg_ref[...])) * dot(hf, wu_ref[...])
    x = x + dot(ff.astype(bf16), wd_ref[...])
    hp = _rms(x, gple_ref[...]).astype(bf16)
    pg = jax.nn.sigmoid(dot(hp, wpg_ref[...]))
    o_ref[...] = x + pg * dot(p_ref[...].astype(bf16), wpp_ref[...])


def _out_stage(x2d, ca, ra, om, gates, p3, l, W, tm):
    M, D = x2d.shape
    rows = lambda n: pl.BlockSpec((tm, n), lambda i: (i, 0))
    wl = lambda a: _const_spec((None,) + a.shape[1:], lambda i: (l,) + (0,) * (a.ndim - 1))
    weights = [W['w_conv_out_b'], W['w_ret_out_b'], W['w_mla_out_b'], W['w_out_b'], W['g_ffn'], W['w_gate_b'],
               W['w_up_b'], W['w_down_b'], W['g_ple'], W['w_ple_gate_b'], W['w_ple_proj_b']]
    return pl.pallas_call(
        _out_kernel,
        grid=(M // tm,),
        in_specs=[rows(D), rows(ca.shape[1]), rows(ra.shape[1]), rows(om.shape[1]), rows(gates.shape[1]),
                  pl.BlockSpec((None, tm, p3.shape[-1]), lambda i: (l, i, 0))] + [wl(a) for a in weights],
        out_specs=rows(D),
        out_shape=jax.ShapeDtypeStruct((M, D), f32),
        compiler_params=_params(("parallel",)),
        name="out_stage",
    )(x2d, ca, ra, om, gates, p3, *weights)


def _rope_tables(pos):
    n = pos.shape[0]
    posf = pos.astype(f32)[:, None]

    def cs(half):
        inv = ROPE_BASE ** (-jnp.arange(half, dtype=f32) / half)
        ang = posf * inv[None, :]
        return jnp.cos(ang), jnp.sin(ang)

    c, s = cs(RET_DK // 2)
    c2r = jnp.concatenate([c, c], axis=-1)
    s2r = jnp.concatenate([-s, s], axis=-1)
    c, s = cs(MLA_ROPE // 2)
    hr = MLA_ROPE // 2
    z = lambda w: jnp.zeros((n, w), f32)
    cm = jnp.concatenate([jnp.ones((n, MLA_NOPE), f32), c, c, z(LANE - MLA_QK)], axis=-1)
    s1m = jnp.concatenate([z(MLA_NOPE + hr), s, z(LANE - MLA_QK)], axis=-1)
    s2m = jnp.concatenate([z(MLA_NOPE), -s, z(LANE - MLA_NOPE - hr)], axis=-1)
    return c2r, s2r, cm, s1m, s2m


def _retention_tables(C):
    log_gamma = jnp.log(1.0 - 2.0 ** (-5.0 - jnp.arange(RET_HEADS, dtype=f32)))
    idx = jnp.arange(C, dtype=f32)
    diff = idx[:, None] - idx[None, :]
    decay = jnp.where(diff >= 0, jnp.exp(jnp.maximum(diff, 0.0)[None] * log_gamma[:, None, None]), 0.0)
    cross = jnp.exp((idx + 1.0)[:, None] * log_gamma[None, :])
    kdec = jnp.exp((C - 1.0 - idx)[:, None] * log_gamma[None, :])
    gc = jnp.exp(C * log_gamma)
    lanes = lambda a: jnp.broadcast_to(a.T[:, :, None], (RET_HEADS, C, LANE))
    return gc, decay, lanes(cross), lanes(kdec)


def _pad_heads(w, heads, width):
    w = w.reshape(w.shape[:-1] + (heads, width))
    w = jnp.pad(w, [(0, 0)] * (w.ndim - 1) + [(0, LANE - width)])
    return w.reshape(w.shape[:-2] + (heads * LANE,))


def kernel(x_prompt, x_sample, p_prompt, p_sample, cache_ckv, cache_kpe, state_ret, state_conv, page_table, g_mix, w_in, w_dw, b_dw, g_cln, b_cln, w_conv_out, g_rgn, w_ret_out, g_qa, w_uq, g_qn, g_kva, w_uk, w_uv, g_kn, w_mla_out, w_out, g_ffn, w_gate, w_up, w_down, g_ple, w_ple_gate, w_ple_proj):
    B, T, D = x_prompt.shape
    Bs, Ts, _ = x_sample.shape
    assert Ts == 1, "the sample group decodes one token per request"
    depth = w_in.shape[0]
    conv_dim, width = w_dw.shape[2], w_dw.shape[1]
    q_lora, kv_lora = g_qa.shape[1], g_kva.shape[1]
    page = cache_ckv.shape[2]
    past_len = page_table.shape[1] * page
    assert T % RET_CHUNK == 0 and width - 1 <= HALO_ROWS <= T

    sizes = (2 * conv_dim, RET_HEADS * RET_DK, RET_HEADS * RET_DK, RET_HEADS * RET_DV, RET_HEADS * RET_DV,
             q_lora, kv_lora, MLA_ROPE, 3 * D)
    src = np.concatenate([[0], np.cumsum(sizes)]).tolist()
    offs = tuple(src[:6])
    kr_slab = jnp.pad(w_in[..., src[7]:src[8]], [(0, 0), (0, 0), (MLA_NOPE, LANE - MLA_QK)])

    row = lambda g: g[:, None, :]
    lane_pad = lambda g: jnp.pad(g, [(0, 0), (0, LANE - g.shape[-1])])[:, None, :]
    W = dict(
        g_mix=row(g_mix), g_qa=row(g_qa), g_kva=row(g_kva),
        w_in_a=w_in[..., :src[5]].astype(bf16),
        w_in_qk=jnp.concatenate([w_in[..., src[5]:src[6]], kr_slab], axis=-1).astype(bf16),
        w_in_kva=w_in[..., src[6]:src[7]].astype(bf16),
        w_in_g=w_in[..., src[8]:].astype(bf16),
        w_uq_p=_pad_heads(w_uq, MLA_HEADS, MLA_QK).astype(bf16),
        g_qn_p=lane_pad(g_qn * (MLA_QK ** -0.5 * LOG2_E)),
        w_uk_p=_pad_heads(w_uk, MLA_HEADS, MLA_NOPE).astype(bf16),
        g_kn_p=lane_pad(g_kn),
        w_uv_b=w_uv.astype(bf16),
        w_ukt_b=jnp.swapaxes(w_uk, 1, 2).astype(bf16),
        w_ukt_p=jnp.swapaxes(_pad_heads(w_uk, MLA_HEADS, MLA_NOPE), 1, 2).astype(bf16),
        w_dw=w_dw, b_dw=row(b_dw), g_cln=row(g_cln), b_cln=row(b_cln), g_rgn=row(g_rgn),
        w_conv_out_b=w_conv_out.astype(bf16), w_ret_out_b=w_ret_out.astype(bf16),
        w_mla_out_b=w_mla_out.astype(bf16), w_out_b=w_out.astype(bf16), g_ffn=row(g_ffn),
        w_gate_b=w_gate.astype(bf16), w_up_b=w_up.astype(bf16), w_down_b=w_down.astype(bf16),
        g_ple=row(g_ple), w_ple_gate_b=w_ple_gate.astype(bf16), w_ple_proj_b=w_ple_proj.astype(bf16),
    )

    tabs_p = _rope_tables(jnp.arange(T, dtype=jnp.int32))
    tabs_s = _rope_tables(jnp.full((Bs,), past_len, jnp.int32))
    ret_tabs = _retention_tables(RET_CHUNK)
    g1 = _retention_tables(1)[0]

    tm = min(ROW_TILE, T)
    tq = min(ATT_TILE, T)
    tr = min(RET_TILE, T)
    n_pages = page_table.shape[1]
    pp = next(c for c in (PAGES_PER_STEP, 8, 4, 2, 1) if n_pages % (2 * c) == 0)
    nr = REQUESTS_PER_STEP if Bs % REQUESTS_PER_STEP == 0 else 1

    cache_kpe_t = jnp.swapaxes(cache_kpe, 2, 3)

    xp = x_prompt.reshape(B * T, D)
    xs = x_sample.reshape(Bs, D)
    pp3 = p_prompt.reshape(depth, B * T, -1)
    ps3 = p_sample.reshape(depth, Bs, -1)
    outs = [[] for _ in range(8)]
    for l in range(depth):
        u, q, k, v, srg, qm, km, vm, ckv, kpe, gates, ca = _in_proj(xp, tabs_p, l, W, offs, tm, T // tm, True)
        s3 = lambda a: a.reshape(B, T, a.shape[-1])
        u3 = s3(u)
        ra, s_new = _ret_prompt(s3(q), s3(k), s3(v), s3(srg), l, W, ret_tabs, tr)
        om = _att_prompt(s3(qm), s3(km), s3(vm), tq)
        xp = _out_stage(xp, ca, ra.reshape(B * T, -1), om.reshape(B * T, -1), gates, pp3, l, W, tm)
        outs[0].append(s3(ckv)); outs[1].append(s3(kpe)); outs[4].append(s_new)
        outs[6].append(u3[:, T - (width - 1):])

        u, q, k, v, srg, qm, km, vm, ckv, kpe, gates = _in_proj(xs, tabs_s, l, W, offs, Bs, 1, False)
        ca = _conv_sample(state_conv, u, l, W)
        ra, s_new = _ret_sample(q, k, v, srg, state_ret, l, W, g1)
        om = _att_sample(qm, km, ckv, cache_ckv, cache_kpe_t, page_table, l, W, pp, nr)
        xs = _out_stage(xs, ca, ra, om, gates, ps3, l, W, Bs)
        outs[2].append(ckv[:, None]); outs[3].append(kpe[:, None]); outs[5].append(s_new)
        outs[7].append(jnp.concatenate([state_conv[l][:, 1:], u[:, None]], axis=1))

    st = [jnp.stack(o) for o in outs]
    return (xp.reshape(B, T, D), xs.reshape(Bs, Ts, D), st[0], st[1], st[2], st[3], st[4], st[5], st[6], st[7])
```

```python
import functools

import jax
import jax.numpy as jnp
import numpy as np
from jax import lax
from jax.experimental import pallas as pl
from jax.experimental.pallas import tpu as pltpu

f32 = jnp.float32
bf16 = jnp.bfloat16

EPS = 1e-6
ROPE_BASE = 10000.0
RET_HEADS = 4
RET_DK = 128
RET_DV = 256
RET_CHUNK = 128
MLA_HEADS = 8
MLA_NOPE = 64
MLA_ROPE = 32
MLA_V = 64
MLA_QK = MLA_NOPE + MLA_ROPE
LANE = 128
SUBLANES = 8
HALO_ROWS = 32
CONV_CHUNK_ROWS = 64
VMEM_LIMIT = 56 * 1024 * 1024
MASK_VALUE = -1e30
LOG2_E = 1.4426950408889634
ROW_TILE = 256
ATT_TILE = 512
RET_TILE = 1024
PAGES_PER_STEP = 32
REQUESTS_PER_STEP = 2

_NT = (((1,), (1,)), ((), ()))
_TN = (((0,), (0,)), ((), ()))


def _rms(x, g):
    return x * lax.rsqrt(jnp.mean(x * x, axis=-1, keepdims=True) + EPS) * g


def _const_spec(shape, index):
    return pl.BlockSpec(shape, index, pipeline_mode=pl.Buffered(1))


def _params(sem):
    return pltpu.CompilerParams(dimension_semantics=sem, vmem_limit_bytes=VMEM_LIMIT)


def _mla_rope(x, cm, s1, s2):
    return x * cm + pltpu.roll(x, MLA_ROPE // 2, 1) * s1 + pltpu.roll(x, LANE - MLA_ROPE // 2, 1) * s2


def _in_proj_kernel(*refs, offs, conv_blocks):
    (x_ref, c2r_ref, s2r_ref, cm_ref, s1m_ref, s2m_ref, gmix_ref, win_ref, wqk_ref, wkva_ref, wg_ref, gqa_ref,
     wuq_ref, gqn_ref, gkva_ref, wuk_ref, wuv_ref, gkn_ref) = refs[:18]
    refs = refs[18:]
    if conv_blocks is not None:
        wdw_ref, bdw_ref, gcln_ref, bcln_ref = refs[:4]
        refs = refs[4:]
    u_ref, q_ref, k_ref, v_ref, srg_ref, qm_ref, km_ref, vm_ref, ckv_ref, kpe_ref, gates_ref = refs[:11]
    refs = refs[11:]
    o_glu, o_rq, o_rk, o_rv, o_rg, o_qa = offs
    if conv_blocks is not None:
        ca_ref, carry, ubuf, sbuf = refs

        @pl.when(pl.program_id(0) == 0)
        def _():
            carry[...] = jnp.zeros_like(carry)

    h = _rms(x_ref[...], gmix_ref[...]).astype(bf16)

    def proj(c0, c1):
        return jnp.dot(h, win_ref[:, c0:c1], preferred_element_type=f32)

    def proj_all(w_ref):
        return jnp.dot(h, w_ref[...], preferred_element_type=f32)

    glu = proj(o_glu, o_rq)
    half = (o_rq - o_glu) // 2
    u = glu[:, :half] * jax.nn.sigmoid(glu[:, half:])
    u_ref[...] = u

    if conv_blocks is not None:
        i = pl.program_id(0)
        tm = u.shape[0]
        ubuf[0:HALO_ROWS, :] = jnp.where(i % conv_blocks > 0, carry[...], 0.0)
        ubuf[HALO_ROWS:, :] = u
        carry[...] = u[tm - HALO_ROWS:, :]
        for step in _conv_ln_swish_steps(ubuf, sbuf, wdw_ref, bdw_ref, gcln_ref, bcln_ref, ca_ref, tm,
                                         wdw_ref.shape[0], min(CONV_CHUNK_ROWS, tm)):
            step()

    c2, s2 = c2r_ref[...], s2r_ref[...]
    rq = proj(o_rq, o_rk)
    rk = proj(o_rk, o_rv)
    for hh in range(RET_HEADS):
        sl = slice(hh * RET_DK, (hh + 1) * RET_DK)
        xq, xk = rq[:, sl], rk[:, sl]
        q_ref[:, sl] = (xq * c2 + pltpu.roll(xq, RET_DK // 2, 1) * s2).astype(bf16)
        k_ref[:, sl] = ((xk * c2 + pltpu.roll(xk, RET_DK // 2, 1) * s2) * (RET_DK ** -0.5)).astype(bf16)
    v_ref[...] = proj(o_rv, o_rg).astype(bf16)
    srg_ref[...] = jax.nn.silu(proj(o_rg, o_qa)).astype(bf16)

    cm, s1, s2m = cm_ref[...], s1m_ref[...], s2m_ref[...]
    qa_kr = proj_all(wqk_ref)
    q_lora = gqa_ref.shape[-1]
    c_q = _rms(qa_kr[:, :q_lora], gqa_ref[...]).astype(bf16)
    qm = jnp.dot(c_q, wuq_ref[...], preferred_element_type=f32)
    gqn = gqn_ref[...]
    for hh in range(MLA_HEADS):
        sl = slice(hh * LANE, (hh + 1) * LANE)
        r = _mla_rope(qm[:, sl], cm, s1, s2m)
        ss = jnp.sum(r * r, axis=-1, keepdims=True)
        qm_ref[:, sl] = (r * lax.rsqrt(ss / MLA_QK + EPS) * gqn).astype(bf16)

    ckv = _rms(proj_all(wkva_ref), gkva_ref[...])
    ckv_ref[...] = ckv
    cb = ckv.astype(bf16)
    kpe_slab = _mla_rope(qa_kr[:, q_lora:], cm, s1, s2m)
    kpe_ref[...] = kpe_slab[:, MLA_NOPE:MLA_QK]
    kn = jnp.dot(cb, wuk_ref[...], preferred_element_type=f32)
    gkn = gkn_ref[...]
    for hh in range(MLA_HEADS):
        sl = slice(hh * LANE, (hh + 1) * LANE)
        kk = kn[:, sl] + kpe_slab
        ss = jnp.sum(kk * kk, axis=-1, keepdims=True)
        km_ref[:, sl] = (kk * lax.rsqrt(ss / MLA_QK + EPS) * gkn).astype(bf16)
    vm_ref[...] = jnp.dot(cb, wuv_ref[...], preferred_element_type=f32).astype(bf16)

    gates_ref[...] = jax.nn.sigmoid(proj_all(wg_ref)).astype(bf16)

def _in_proj(x2d, tabs, l, W, offs, tm, n_pos_blocks, fuse_conv):
    M, D = x2d.shape
    conv_dim = (offs[1] - offs[0]) // 2
    n_gates = W['w_in_g'].shape[-1]
    kv_lora = W['w_in_kva'].shape[-1]
    rows = lambda n: pl.BlockSpec((tm, n), lambda i: (i, 0))
    tab = pl.BlockSpec((tm, LANE), lambda i: (i % n_pos_blocks, 0))
    wl = lambda a: _const_spec((None,) + a.shape[1:], lambda i: (l,) + (0,) * (a.ndim - 1))
    out_dims = [(conv_dim, f32), (RET_HEADS * RET_DK, bf16), (RET_HEADS * RET_DK, bf16), (RET_HEADS * RET_DV, bf16),
                (RET_HEADS * RET_DV, bf16), (MLA_HEADS * LANE, bf16), (MLA_HEADS * LANE, bf16),
                (MLA_HEADS * MLA_V, bf16), (kv_lora, f32), (MLA_ROPE, f32), (n_gates, bf16)]
    weights = [W['g_mix'], W['w_in_a'], W['w_in_qk'], W['w_in_kva'], W['w_in_g'], W['g_qa'], W['w_uq_p'],
               W['g_qn_p'], W['g_kva'], W['w_uk_p'], W['w_uv_b'],
               W['g_kn_p']]
    scratch = []
    if fuse_conv:
        assert tm >= HALO_ROWS
        weights += [W['w_dw'], W['b_dw'], W['g_cln'], W['b_cln']]
        out_dims.append((conv_dim, bf16))
        scratch = [pltpu.VMEM((HALO_ROWS, conv_dim), f32), pltpu.VMEM((HALO_ROWS + tm, conv_dim), f32),
                   pltpu.VMEM((SUBLANES, HALO_ROWS + tm, conv_dim), f32)]
    return pl.pallas_call(
        functools.partial(_in_proj_kernel, offs=offs, conv_blocks=n_pos_blocks if fuse_conv else None),
        grid=(M // tm,),
        in_specs=[rows(D)] + [tab] * 5 + [wl(a) for a in weights],
        out_specs=[rows(n) for n, _ in out_dims],
        out_shape=[jax.ShapeDtypeStruct((M, n), dt) for n, dt in out_dims],
        scratch_shapes=scratch,
        compiler_params=_params(("arbitrary",) if fuse_conv else ("parallel",)),
        name="in_proj",
    )(x2d, *tabs, *weights)


def _ln_swish(y, g, b):
    mu = jnp.mean(y, axis=-1, keepdims=True)
    d = y - mu
    var = jnp.mean(d * d, axis=-1, keepdims=True)
    z = d * lax.rsqrt(var + EPS) * g + b
    return z * jax.nn.sigmoid(z)


def _conv_ln_swish_steps(ubuf, sbuf, w_ref, b_ref, g_ref, bl_ref, o_ref, tc, width, rc):
    first = HALO_ROWS - (width - 1)
    C = ubuf.shape[-1]
    R = HALO_ROWS + tc

    def shifted_copies():
        sbuf[0] = ubuf[...]
        for ph in range(1, SUBLANES):
            sbuf[ph, 0:R - SUBLANES, :] = ubuf[ph:R - SUBLANES + ph, :]

    def chunk(r0):
        cols = []
        for c0 in range(0, C, LANE):
            acc = jnp.zeros((rc, LANE), f32) + b_ref[:, c0:c0 + LANE]
            for j in range(width):
                ph = (first + j) % SUBLANES
                base = first + j - ph + r0
                acc = acc + w_ref[j:j + 1, c0:c0 + LANE] * sbuf[ph, base:base + rc, c0:c0 + LANE]
            cols.append(acc)
        y = jnp.concatenate(cols, axis=-1)
        o_ref[r0:r0 + rc, :] = _ln_swish(y, g_ref[...], bl_ref[...]).astype(bf16)

    return [shifted_copies] + [functools.partial(chunk, r0) for r0 in range(0, tc, rc)]


def _conv_sample_kernel(st_ref, u_ref, w_ref, b_ref, g_ref, bl_ref, o_ref, *, width):
    y = u_ref[...] * w_ref[width - 1:width, :] + b_ref[...]
    for j in range(width - 1):
        y = y + st_ref[:, j, :] * w_ref[j:j + 1, :]
    o_ref[...] = _ln_swish(y, g_ref[...], bl_ref[...]).astype(bf16)


def _conv_sample(state, u2, l, W):
    Bs, C = u2.shape
    width = W['w_dw'].shape[1]
    wl = lambda a: pl.BlockSpec((None,) + a.shape[1:], lambda i: (l,) + (0,) * (a.ndim - 1))
    weights = [W['w_dw'], W['b_dw'], W['g_cln'], W['b_cln']]
    return pl.pallas_call(
        functools.partial(_conv_sample_kernel, width=width),
        grid=(1,),
        in_specs=[pl.BlockSpec((None,) + state.shape[1:], lambda i: (l, 0, 0, 0)),
                  pl.BlockSpec(u2.shape, lambda i: (0, 0))] + [wl(a) for a in weights],
        out_specs=pl.BlockSpec((Bs, C), lambda i: (0, 0)),
        out_shape=jax.ShapeDtypeStruct((Bs, C), bf16),
        compiler_params=_params(("arbitrary",)),
        name="conv_sample",
    )(state, u2, *weights)


def _group_norm_gate(o, g, gate):
    mu = jnp.mean(o, axis=-1, keepdims=True)
    d = o - mu
    var = jnp.mean(d * d, axis=-1, keepdims=True)
    return (d * lax.rsqrt(var + EPS) * g * gate).astype(bf16)


def _ret_prompt_kernel(gc_ref, q_ref, k_ref, v_ref, srg_ref, decay_ref, cross_ref, kdec_ref, g_ref,
                       o_ref, s_out_ref, s_ref):
    c = pl.program_id(1)

    @pl.when(c == 0)
    def _():
        s_ref[...] = jnp.zeros_like(s_ref)

    C = RET_CHUNK
    for hh in range(RET_HEADS):
        ks = slice(hh * RET_DK, (hh + 1) * RET_DK)
        vs = slice(hh * RET_DV, (hh + 1) * RET_DV)
        cr = jnp.concatenate([cross_ref[hh]] * (RET_DV // LANE), axis=-1)
        s_cur = s_ref[hh]
        for r0 in range(0, q_ref.shape[1], C):
            rows = slice(r0, r0 + C)
            q, k, v = q_ref[0, rows, ks], k_ref[0, rows, ks], v_ref[0, rows, vs]
            scores = lax.dot_general(q, k, _NT, preferred_element_type=f32) * decay_ref[hh]
            o = jnp.dot(scores.astype(bf16), v, preferred_element_type=f32)
            o = o + jnp.dot(q, s_cur.astype(bf16), preferred_element_type=f32) * cr
            kd = (k.astype(f32) * kdec_ref[hh]).astype(bf16)
            s_cur = gc_ref[hh] * s_cur + lax.dot_general(kd, v, _TN, preferred_element_type=f32)
            o_ref[0, rows, vs] = _group_norm_gate(o, g_ref[:, vs], srg_ref[0, rows, vs].astype(f32))
        s_ref[hh] = s_cur

    @pl.when(c == pl.num_programs(1) - 1)
    def _():
        s_out_ref[0] = s_ref[...]


def _ret_prompt(q3, k3, v3, srg3, l, W, tabs, tr):
    B, T, _ = q3.shape
    gc, decay, cross, kdec = tabs
    blk = lambda n: pl.BlockSpec((1, tr, n), lambda b, c: (b, c, 0))
    tab = _const_spec((RET_HEADS, RET_CHUNK, LANE), lambda b, c: (0, 0, 0))
    return pl.pallas_call(
        _ret_prompt_kernel,
        grid=(B, T // tr),
        in_specs=[pl.BlockSpec(memory_space=pltpu.SMEM), blk(RET_HEADS * RET_DK), blk(RET_HEADS * RET_DK),
                  blk(RET_HEADS * RET_DV), blk(RET_HEADS * RET_DV), tab, tab, tab,
                  _const_spec((None, 1, RET_HEADS * RET_DV), lambda b, c: (l, 0, 0))],
        out_specs=[blk(RET_HEADS * RET_DV),
                   pl.BlockSpec((1, RET_HEADS, RET_DK, RET_DV), lambda b, c: (b, 0, 0, 0))],
        out_shape=[jax.ShapeDtypeStruct((B, T, RET_HEADS * RET_DV), bf16),
                   jax.ShapeDtypeStruct((B, RET_HEADS, RET_DK, RET_DV), f32)],
        scratch_shapes=[pltpu.VMEM((RET_HEADS, RET_DK, RET_DV), f32)],
        compiler_params=_params(("parallel", "arbitrary")),
        name="ret_prompt",
    )(gc, q3, k3, v3, srg3, decay, cross, kdec, W['g_rgn'])


def _row_to_col(row, n):
    eye = lax.broadcasted_iota(jnp.int32, (n, n), 0) == lax.broadcasted_iota(jnp.int32, (n, n), 1)
    return jnp.sum(jnp.where(eye, jnp.broadcast_to(row, (n, n)), 0.0), axis=-1, keepdims=True)


def _ret_sample_kernel(g1_ref, q_ref, k_ref, v_ref, srg_ref, s_ref, g_ref, o_ref, s_out_ref):
    for i in range(q_ref.shape[0]):
        for hh in range(RET_HEADS):
            ks = slice(hh * RET_DK, (hh + 1) * RET_DK)
            vs = slice(hh * RET_DV, (hh + 1) * RET_DV)
            qcol = _row_to_col(q_ref[i, :, ks].astype(f32), RET_DK)
            kcol = _row_to_col(k_ref[i, :, ks].astype(f32), RET_DK)
            v = v_ref[i, :, vs].astype(f32)
            s_new = g1_ref[hh] * s_ref[i, hh] + kcol * v
            s_out_ref[i, hh] = s_new
            o = jnp.sum(qcol * s_new, axis=0, keepdims=True)
            o_ref[i, :, vs] = _group_norm_gate(o, g_ref[:, vs], srg_ref[i, :, vs].astype(f32))


def _ret_sample(q2, k2, v2, srg2, state, l, W, g1):
    Bs = q2.shape[0]
    nb = next(c for c in (4, 2, 1) if Bs % c == 0)
    r3 = lambda a: a.reshape(Bs, 1, a.shape[-1])
    row = lambda n: pl.BlockSpec((nb, 1, n), lambda b: (b, 0, 0))
    sblk = (nb, RET_HEADS, RET_DK, RET_DV)
    o, s_new = pl.pallas_call(
        _ret_sample_kernel,
        grid=(Bs // nb,),
        in_specs=[pl.BlockSpec(memory_space=pltpu.SMEM), row(RET_HEADS * RET_DK), row(RET_HEADS * RET_DK),
                  row(RET_HEADS * RET_DV), row(RET_HEADS * RET_DV),
                  pl.BlockSpec((None,) + sblk, lambda b: (l, b, 0, 0, 0)),
                  pl.BlockSpec((None, 1, RET_HEADS * RET_DV), lambda b: (l, 0, 0))],
        out_specs=[row(RET_HEADS * RET_DV), pl.BlockSpec(sblk, lambda b: (b, 0, 0, 0))],
        out_shape=[jax.ShapeDtypeStruct((Bs, 1, RET_HEADS * RET_DV), bf16),
                   jax.ShapeDtypeStruct((Bs, RET_HEADS, RET_DK, RET_DV), f32)],
        compiler_params=_params(("parallel",)),
        name="ret_sample",
    )(g1, r3(q2), r3(k2), r3(v2), r3(srg2), state, W['g_rgn'])
    return o.reshape(Bs, -1), s_new


def _att_prompt_kernel(qi_ref, ki_ref, q_ref, k_ref, v_ref, o_ref, m_ref, l_ref, acc_ref):
    s_idx = pl.program_id(1)
    qi, ki = qi_ref[s_idx], ki_ref[s_idx]
    tq, tk = q_ref.shape[1], k_ref.shape[1]
    n_lt = tk // LANE

    @pl.when(ki == 0)
    def _():
        m_ref[...] = jnp.full_like(m_ref, MASK_VALUE)
        l_ref[...] = jnp.zeros_like(l_ref)
        acc_ref[...] = jnp.zeros_like(acc_ref)

    def step(diagonal):
        for hd in range(MLA_HEADS):
            v = v_ref[0, :, (hd // 2) * LANE:(hd // 2 + 1) * LANE]
            sl = slice(hd * LANE, (hd + 1) * LANE)
            s = lax.dot_general(q_ref[0, :, sl], k_ref[0, :, sl], _NT, preferred_element_type=f32)
            if diagonal:
                row = lax.broadcasted_iota(jnp.int32, (tq, tk), 0)
                col = lax.broadcasted_iota(jnp.int32, (tq, tk), 1)
                s = jnp.where(col <= row, s, MASK_VALUE)
            m_prev = m_ref[hd]
            smax = s[:, 0:LANE]
            for c in range(1, n_lt):
                smax = jnp.maximum(smax, s[:, c * LANE:(c + 1) * LANE])
            m_new = jnp.maximum(m_prev, jnp.max(smax, axis=-1, keepdims=True))
            a = jnp.exp2(m_prev - m_new)
            p = jnp.exp2(s - jnp.concatenate([m_new] * n_lt, axis=-1))
            psum = p[:, 0:LANE]
            for c in range(1, n_lt):
                psum = psum + p[:, c * LANE:(c + 1) * LANE]
            l_ref[hd] = a * l_ref[hd] + psum
            acc_ref[hd] = a * acc_ref[hd] + jnp.dot(p.astype(bf16), v, preferred_element_type=f32)
            m_ref[hd] = m_new

    @pl.when(ki < qi)
    def _():
        step(False)

    @pl.when(ki == qi)
    def _():
        step(True)
        lane = lax.broadcasted_iota(jnp.int32, (tq, LANE), 1)
        for g in range(MLA_HEADS // 2):
            l0 = jnp.sum(l_ref[2 * g], axis=-1, keepdims=True)
            l1 = jnp.sum(l_ref[2 * g + 1], axis=-1, keepdims=True)
            o = jnp.where(lane < MLA_V, acc_ref[2 * g] / l0, acc_ref[2 * g + 1] / l1)
            o_ref[0, :, g * LANE:(g + 1) * LANE] = o.astype(bf16)


def _att_prompt(qm3, km3, vm3, tq):
    B, T, _ = qm3.shape
    nq = T // tq
    H = MLA_HEADS
    pairs = [(qi, ki) for qi in range(nq) for ki in range(qi + 1)]
    qi_tab = jnp.asarray([p[0] for p in pairs], jnp.int32)
    ki_tab = jnp.asarray([p[1] for p in pairs], jnp.int32)
    grid_spec = pltpu.PrefetchScalarGridSpec(
        num_scalar_prefetch=2,
        grid=(B, len(pairs)),
        in_specs=[pl.BlockSpec((1, tq, H * LANE), lambda b, s, qi, ki: (b, qi[s], 0)),
                  pl.BlockSpec((1, tq, H * LANE), lambda b, s, qi, ki: (b, ki[s], 0)),
                  pl.BlockSpec((1, tq, H * MLA_V), lambda b, s, qi, ki: (b, ki[s], 0))],
        out_specs=pl.BlockSpec((1, tq, H * MLA_V), lambda b, s, qi, ki: (b, qi[s], 0)),
        scratch_shapes=[pltpu.VMEM((H, tq, LANE), f32), pltpu.VMEM((H, tq, LANE), f32),
                        pltpu.VMEM((H, tq, LANE), f32)],
    )
    return pl.pallas_call(
        _att_prompt_kernel,
        grid_spec=grid_spec,
        out_shape=jax.ShapeDtypeStruct((B, T, H * MLA_V), bf16),
        compiler_params=_params(("parallel", "arbitrary")),
        name="att_prompt",
    )(qi_tab, ki_tab, qm3, km3, vm3)


def _hi_lo(x):
    hi = x.astype(bf16).astype(f32)
    return jnp.concatenate([hi, x - hi], axis=0).astype(bf16)


def _rows_select(rows, n=None):
    n, w = n or len(rows), rows[0].shape[-1]
    sub = lax.broadcasted_iota(jnp.int32, (n, w), 0)
    out = jnp.zeros((n, w), rows[0].dtype)
    for i, r in enumerate(rows):
        out = jnp.where(sub == i, jnp.broadcast_to(r, (n, w)), out)
    return out


def _att_sample_kernel(pt_ref, ckv_hbm, kpe_hbm, qm_ref, kmn_ref, ckvn_ref, gkn_ref, wukt_ref, wuktp_ref, wuv_ref,
                       o_ref, ckv_buf, kpe_buf, sem_c, sem_k, aq_ref, cbb_ref, *, l, pp, nj, nr):
    b = pl.program_id(0)
    nb = pl.num_programs(0)
    H = MLA_HEADS
    page = ckv_hbm.shape[2]

    def copies(bb, r, j, slot):
        out = []
        for i in range(pp):
            pg = pt_ref[bb * nr + r, j * pp + i]
            out.append(pltpu.make_async_copy(ckv_hbm.at[l, pg], ckv_buf.at[r, slot, pl.ds(i * page, page)],
                                             sem_c.at[r, slot]))
            out.append(pltpu.make_async_copy(kpe_hbm.at[l, pg], kpe_buf.at[r, slot, :, pl.ds(i * page, page)],
                                             sem_k.at[r, slot]))
        return out

    @pl.when(b == 0)
    def _():
        for r in range(nr):
            for c in copies(0, r, 0, 0) + copies(0, r, 1, 1):
                c.start()

    gkn = jnp.concatenate([gkn_ref[...]] * H, axis=-1)
    sub = lax.broadcasted_iota(jnp.int32, (2 * H, H * LANE), 0)
    head = lax.broadcasted_iota(jnp.int32, (2 * H, H * LANE), 1) // LANE
    qr = []
    for r in range(nr):
        qg = qm_ref[r].astype(f32) * gkn
        qbd = jnp.where(sub == head, jnp.broadcast_to(qg, (2 * H, H * LANE)), 0.0).astype(bf16)
        qabs = jnp.dot(qbd, wuktp_ref[...], preferred_element_type=f32)[0:H]
        aq_ref[r, 0:H * MLA_NOPE, :] = wukt_ref[...]
        aq_ref[r, H * MLA_NOPE:, :] = _hi_lo(qabs)
        qr.append(_rows_select([qg[:, h * LANE + MLA_NOPE:h * LANE + MLA_QK] for h in range(H)],
                               2 * H).astype(bf16))

    def scores(r, j):
        slot = j % 2
        cbb_ref[r, slot] = ckv_buf[r, slot].astype(bf16)
        cb = cbb_ref[r, slot]
        kt = kpe_buf[r, slot]
        x = lax.dot_general(aq_ref[r], cb, _NT, preferred_element_type=f32)
        ss = jnp.concatenate(
            [jnp.sum(jnp.square(x[h * MLA_NOPE:(h + 1) * MLA_NOPE]), axis=0, keepdims=True) for h in range(H)],
            axis=0)
        ss_rope = jnp.sum(kt * kt, axis=0, keepdims=True)
        s_rope = jnp.dot(qr[r], kt.astype(bf16), preferred_element_type=f32)[0:H]
        s_nope = x[H * MLA_NOPE:H * MLA_NOPE + H] + x[H * MLA_NOPE + H:]
        s = (s_nope + s_rope) * lax.rsqrt((ss + ss_rope) / MLA_QK + EPS)
        if j + 2 < nj:
            for c in copies(b, r, j + 2, slot):
                c.start()
        else:
            @pl.when(b + 1 < nb)
            def _():
                for c in copies(b + 1, r, j + 2 - nj, slot):
                    c.start()
        return s

    def step_scores(j):
        for r in range(nr):
            for c in copies(b, r, j, j % 2):
                c.wait()
        return [scores(r, j) for r in range(nr)]

    def accumulate(r, j, s, m, lsum, ctx):
        m_new = jnp.maximum(m, jnp.max(s, axis=-1, keepdims=True))
        a = jnp.exp2(m - m_new)
        p = jnp.exp2(s - m_new)
        lsum = a * lsum + jnp.sum(p, axis=-1, keepdims=True)
        p16 = jnp.concatenate([p, jnp.zeros_like(p)], axis=0).astype(bf16)
        ctx = a * ctx + jnp.dot(p16, cbb_ref[r, j % 2], preferred_element_type=f32)[0:H]
        return m_new, lsum, ctx

    state = [(jnp.full((H, 1), MASK_VALUE, f32), jnp.zeros((H, 1), f32), jnp.zeros((H, ckvn_ref.shape[-1]), f32))
             for _ in range(nr)]
    s_prev = step_scores(0)
    for j in range(1, nj):
        s_cur = step_scores(j)
        state = [accumulate(r, j - 1, s_prev[r], *state[r]) for r in range(nr)]
        s_prev = s_cur
    state = [accumulate(r, nj - 1, s_prev[r], *state[r]) for r in range(nr)]

    for r in range(nr):
        m, lsum, ctx = state[r]
        prod = qm_ref[r].astype(f32) * kmn_ref[r].astype(f32)
        s_new = jnp.sum(_rows_select([prod[:, h * LANE:(h + 1) * LANE] for h in range(H)]), axis=-1,
                        keepdims=True)
        m_fin = jnp.maximum(m, s_new)
        a = jnp.exp2(m - m_fin)
        pn = jnp.exp2(s_new - m_fin)
        l_fin = a * lsum + pn
        c_new = ckvn_ref[r].astype(bf16).astype(f32)
        ctx = (a * ctx + pn.astype(bf16).astype(f32) * c_new) / l_fin
        both = jnp.dot(_hi_lo(ctx), wuv_ref[...], preferred_element_type=f32)
        full = both[0:H] + both[H:]
        sub_o = lax.broadcasted_iota(jnp.int32, full.shape, 0)
        head_o = lax.broadcasted_iota(jnp.int32, full.shape, 1) // MLA_V
        o_ref[r] = jnp.sum(jnp.where(sub_o == head_o, full, 0.0), axis=0, keepdims=True).astype(bf16)


def _att_sample(qm2, kmn2, ckvn2, cache_ckv, cache_kpe_t, page_table, l, W, pp, nr):
    Bs = qm2.shape[0]
    n_pages = page_table.shape[1]
    page, n_lat = cache_ckv.shape[2], cache_ckv.shape[3]
    rope = cache_kpe_t.shape[2]
    H = MLA_HEADS
    nj = n_pages // pp
    assert nj % 2 == 0, "the two staging slots alternate by step parity across grid steps"
    assert Bs % nr == 0
    r3 = lambda a: a.reshape(Bs, 1, a.shape[-1])
    row = lambda n: pl.BlockSpec((nr, 1, n), lambda b, pt: (b, 0, 0))
    wl = lambda a: _const_spec((None,) + a.shape[1:], lambda b, pt: (l,) + (0,) * (a.ndim - 1))
    weights = [W['g_kn_p'], W['w_ukt_b'], W['w_ukt_p'], W['w_uv_b']]
    grid_spec = pltpu.PrefetchScalarGridSpec(
        num_scalar_prefetch=1,
        grid=(Bs // nr,),
        in_specs=([pl.BlockSpec(memory_space=pl.ANY), pl.BlockSpec(memory_space=pl.ANY)]
                  + [row(H * LANE), row(H * LANE), row(n_lat)] + [wl(a) for a in weights]),
        out_specs=row(H * MLA_V),
        scratch_shapes=[pltpu.VMEM((nr, 2, pp * page, n_lat), f32), pltpu.VMEM((nr, 2, rope, pp * page), f32),
                        pltpu.SemaphoreType.DMA((nr, 2)), pltpu.SemaphoreType.DMA((nr, 2)),
                        pltpu.VMEM((nr, H * MLA_NOPE + 2 * H, n_lat), bf16),
                        pltpu.VMEM((nr, 2, pp * page, n_lat), bf16)],
    )
    o = pl.pallas_call(
        functools.partial(_att_sample_kernel, l=l, pp=pp, nj=nj, nr=nr),
        grid_spec=grid_spec,
        out_shape=jax.ShapeDtypeStruct((Bs, 1, H * MLA_V), bf16),
        compiler_params=_params(("arbitrary",)),
        name="att_sample",
    )(page_table, cache_ckv, cache_kpe_t, r3(qm2), r3(kmn2), r3(ckvn2), *weights)
    return o.reshape(Bs, -1)


def _out_kernel(x_ref, ca_ref, ra_ref, om_ref, gates_ref, p_ref, wco_ref, wro_ref, wmo_ref, wout_ref, gffn_ref,
                wg_ref, wu_ref, wd_ref, gple_ref, wpg_ref, wpp_ref, o_ref):
    D = x_ref.shape[-1]
    dot = lambda a, b: jnp.dot(a, b, preferred_element_type=f32)
    gate = lambda i: gates_ref[:, i * D:(i + 1) * D].astype(f32)
    mix = (gate(0) * dot(ca_ref[...], wco_ref[...]) + gate(1) * dot(ra_ref[...], wro_ref[...])
           + gate(2) * dot(om_ref[...], wmo_ref[...]))
    x = x_ref[...] + dot(mix.astype(bf16), wout_ref[...])
    hf = _rms(x, gffn_ref[...]).astype(bf16)
    ff = jax.nn.silu(dot(hf, wg_ref[...])) * dot(hf, wu_ref[...])
    x = x + dot(ff.astype(bf16), wd_ref[...])
    hp = _rms(x, gple_ref[...]).astype(bf16)
    pg = jax.nn.sigmoid(dot(hp, wpg_ref[...]))
    o_ref[...] = x + pg * dot(p_ref[...].astype(bf16), wpp_ref[...])


def _out_stage(x2d, ca, ra, om, gates, p3, l, W, tm):
    M, D = x2d.shape
    rows = lambda n: pl.BlockSpec((tm, n), lambda i: (i, 0))
    wl = lambda a: _const_spec((None,) + a.shape[1:], lambda i: (l,) + (0,) * (a.ndim - 1))
    weights = [W['w_conv_out_b'], W['w_ret_out_b'], W['w_mla_out_b'], W['w_out_b'], W['g_ffn'], W['w_gate_b'],
               W['w_up_b'], W['w_down_b'], W['g_ple'], W['w_ple_gate_b'], W['w_ple_proj_b']]
    return pl.pallas_call(
        _out_kernel,
        grid=(M // tm,),
        in_specs=[rows(D), rows(ca.shape[1]), rows(ra.shape[1]), rows(om.shape[1]), rows(gates.shape[1]),
                  pl.BlockSpec((None, tm, p3.shape[-1]), lambda i: (l, i, 0))] + [wl(a) for a in weights],
        out_specs=rows(D),
        out_shape=jax.ShapeDtypeStruct((M, D), f32),
        compiler_params=_params(("parallel",)),
        name="out_stage",
    )(x2d, ca, ra, om, gates, p3, *weights)


def _rope_tables(pos):
    n = pos.shape[0]
    posf = pos.astype(f32)[:, None]

    def cs(half):
        inv = ROPE_BASE ** (-jnp.arange(half, dtype=f32) / half)
        ang = posf * inv[None, :]
        return jnp.cos(ang), jnp.sin(ang)

    c, s = cs(RET_DK // 2)
    c2r = jnp.concatenate([c, c], axis=-1)
    s2r = jnp.concatenate([-s, s], axis=-1)
    c, s = cs(MLA_ROPE // 2)
    hr = MLA_ROPE // 2
    z = lambda w: jnp.zeros((n, w), f32)
    cm = jnp.concatenate([jnp.ones((n, MLA_NOPE), f32), c, c, z(LANE - MLA_QK)], axis=-1)
    s1m = jnp.concatenate([z(MLA_NOPE + hr), s, z(LANE - MLA_QK)], axis=-1)
    s2m = jnp.concatenate([z(MLA_NOPE), -s, z(LANE - MLA_NOPE - hr)], axis=-1)
    return c2r, s2r, cm, s1m, s2m


def _retention_tables(C):
    log_gamma = jnp.log(1.0 - 2.0 ** (-5.0 - jnp.arange(RET_HEADS, dtype=f32)))
    idx = jnp.arange(C, dtype=f32)
    diff = idx[:, None] - idx[None, :]
    decay = jnp.where(diff >= 0, jnp.exp(jnp.maximum(diff, 0.0)[None] * log_gamma[:, None, None]), 0.0)
    cross = jnp.exp((idx + 1.0)[:, None] * log_gamma[None, :])
    kdec = jnp.exp((C - 1.0 - idx)[:, None] * log_gamma[None, :])
    gc = jnp.exp(C * log_gamma)
    lanes = lambda a: jnp.broadcast_to(a.T[:, :, None], (RET_HEADS, C, LANE))
    return gc, decay, lanes(cross), lanes(kdec)


def _pad_heads(w, heads, width):
    w = w.reshape(w.shape[:-1] + (heads, width))
    w = jnp.pad(w, [(0, 0)] * (w.ndim - 1) + [(0, LANE - width)])
    return w.reshape(w.shape[:-2] + (heads * LANE,))


def kernel(x_prompt, x_sample, p_prompt, p_sample, cache_ckv, cache_kpe, state_ret, state_conv, page_table, g_mix, w_in, w_dw, b_dw, g_cln, b_cln, w_conv_out, g_rgn, w_ret_out, g_qa, w_uq, g_qn, g_kva, w_uk, w_uv, g_kn, w_mla_out, w_out, g_ffn, w_gate, w_up, w_down, g_ple, w_ple_gate, w_ple_proj):
    B, T, D = x_prompt.shape
    Bs, Ts, _ = x_sample.shape
    assert Ts == 1, "the sample group decodes one token per request"
    depth = w_in.shape[0]
    conv_dim, width = w_dw.shape[2], w_dw.shape[1]
    q_lora, kv_lora = g_qa.shape[1], g_kva.shape[1]
    page = cache_ckv.shape[2]
    past_len = page_table.shape[1] * page
    assert T % RET_CHUNK == 0 and width - 1 <= HALO_ROWS <= T

    sizes = (2 * conv_dim, RET_HEADS * RET_DK, RET_HEADS * RET_DK, RET_HEADS * RET_DV, RET_HEADS * RET_DV,
             q_lora, kv_lora, MLA_ROPE, 3 * D)
    src = np.concatenate([[0], np.cumsum(sizes)]).tolist()
    offs = tuple(src[:6])
    kr_slab = jnp.pad(w_in[..., src[7]:src[8]], [(0, 0), (0, 0), (MLA_NOPE, LANE - MLA_QK)])

    row = lambda g: g[:, None, :]
    lane_pad = lambda g: jnp.pad(g, [(0, 0), (0, LANE - g.shape[-1])])[:, None, :]
    W = dict(
        g_mix=row(g_mix), g_qa=row(g_qa), g_kva=row(g_kva),
        w_in_a=w_in[..., :src[5]].astype(bf16),
        w_in_qk=jnp.concatenate([w_in[..., src[5]:src[6]], kr_slab], axis=-1).astype(bf16),
        w_in_kva=w_in[..., src[6]:src[7]].astype(bf16),
        w_in_g=w_in[..., src[8]:].astype(bf16),
        w_uq_p=_pad_heads(w_uq, MLA_HEADS, MLA_QK).astype(bf16),
        g_qn_p=lane_pad(g_qn * (MLA_QK ** -0.5 * LOG2_E)),
        w_uk_p=_pad_heads(w_uk, MLA_HEADS, MLA_NOPE).astype(bf16),
        g_kn_p=lane_pad(g_kn),
        w_uv_b=w_uv.astype(bf16),
        w_ukt_b=jnp.swapaxes(w_uk, 1, 2).astype(bf16),
        w_ukt_p=jnp.swapaxes(_pad_heads(w_uk, MLA_HEADS, MLA_NOPE), 1, 2).astype(bf16),
        w_dw=w_dw, b_dw=row(b_dw), g_cln=row(g_cln), b_cln=row(b_cln), g_rgn=row(g_rgn),
        w_conv_out_b=w_conv_out.astype(bf16), w_ret_out_b=w_ret_out.astype(bf16),
        w_mla_out_b=w_mla_out.astype(bf16), w_out_b=w_out.astype(bf16), g_ffn=row(g_ffn),
        w_gate_b=w_gate.astype(bf16), w_up_b=w_up.astype(bf16), w_down_b=w_down.astype(bf16),
        g_ple=row(g_ple), w_ple_gate_b=w_ple_gate.astype(bf16), w_ple_proj_b=w_ple_proj.astype(bf16),
    )

    tabs_p = _rope_tables(jnp.arange(T, dtype=jnp.int32))
    tabs_s = _rope_tables(jnp.full((Bs,), past_len, jnp.int32))
    ret_tabs = _retention_tables(RET_CHUNK)
    g1 = _retention_tables(1)[0]

    tm = min(ROW_TILE, T)
    tq = min(ATT_TILE, T)
    tr = min(RET_TILE, T)
    n_pages = page_table.shape[1]
    pp = next(c for c in (PAGES_PER_STEP, 8, 4, 2, 1) if n_pages % (2 * c) == 0)
    nr = REQUESTS_PER_STEP if Bs % REQUESTS_PER_STEP == 0 else 1

    cache_kpe_t = jnp.swapaxes(cache_kpe, 2, 3)

    xp = x_prompt.reshape(B * T, D)
    xs = x_sample.reshape(Bs, D)
    pp3 = p_prompt.reshape(depth, B * T, -1)
    ps3 = p_sample.reshape(depth, Bs, -1)
    outs = [[] for _ in range(8)]
    for l in range(depth):
        u, q, k, v, srg, qm, km, vm, ckv, kpe, gates, ca = _in_proj(xp, tabs_p, l, W, offs, tm, T // tm, True)
        s3 = lambda a: a.reshape(B, T, a.shape[-1])
        u3 = s3(u)
        ra, s_new = _ret_prompt(s3(q), s3(k), s3(v), s3(srg), l, W, ret_tabs, tr)
        om = _att_prompt(s3(qm), s3(km), s3(vm), tq)
        xp = _out_stage(xp, ca, ra.reshape(B * T, -1), om.reshape(B * T, -1), gates, pp3, l, W, tm)
        outs[0].append(s3(ckv)); outs[1].append(s3(kpe)); outs[4].append(s_new)
        outs[6].append(u3[:, T - (width - 1):])

        u, q, k, v, srg, qm, km, vm, ckv, kpe, gates = _in_proj(xs, tabs_s, l, W, offs, Bs, 1, False)
        ca = _conv_sample(state_conv, u, l, W)
        ra, s_new = _ret_sample(q, k, v, srg, state_ret, l, W, g1)
        om = _att_sample(qm, km, ckv, cache_ckv, cache_kpe_t, page_table, l, W, pp, nr)
        xs = _out_stage(xs, ca, ra, om, gates, ps3, l, W, Bs)
        outs[2].append(ckv[:, None]); outs[3].append(kpe[:, None]); outs[5].append(s_new)
        outs[7].append(jnp.concatenate([state_conv[l][:, 1:], u[:, None]], axis=1))

    st = [jnp.stack(o) for o in outs]
    return (xp.reshape(B, T, D), xs.reshape(Bs, Ts, D), st[0], st[1], st[2], st[3], st[4], st[5], st[6], st[7])
```

```python
import functools

import jax
import jax.numpy as jnp
import numpy as np
from jax import lax
from jax.experimental import pallas as pl
from jax.experimental.pallas import tpu as pltpu

f32 = jnp.float32
bf16 = jnp.bfloat16

EPS = 1e-6
ROPE_BASE = 10000.0
RET_HEADS = 4
RET_DK = 128
RET_DV = 256
RET_CHUNK = 128
MLA_HEADS = 8
MLA_NOPE = 64
MLA_ROPE = 32
MLA_V = 64
MLA_QK = MLA_NOPE + MLA_ROPE
LANE = 128
SUBLANES = 8
HALO_ROWS = 32
CONV_CHUNK_ROWS = 64
VMEM_LIMIT = 56 * 1024 * 1024
MASK_VALUE = -1e30
LOG2_E = 1.4426950408889634
ROW_TILE = 256
ATT_TILE = 512
RET_TILE = 1024
PAGES_PER_STEP = 32
REQUESTS_PER_STEP = 2

_NT = (((1,), (1,)), ((), ()))
_TN = (((0,), (0,)), ((), ()))


def _rms(x, g):
    return x * lax.rsqrt(jnp.mean(x * x, axis=-1, keepdims=True) + EPS) * g


def _const_spec(shape, index):
    return pl.BlockSpec(shape, index, pipeline_mode=pl.Buffered(1))


def _params(sem):
    return pltpu.CompilerParams(dimension_semantics=sem, vmem_limit_bytes=VMEM_LIMIT)


def _mla_rope(x, cm, s1, s2):
    return x * cm + pltpu.roll(x, MLA_ROPE // 2, 1) * s1 + pltpu.roll(x, LANE - MLA_ROPE // 2, 1) * s2


def _in_proj_kernel(*refs, offs, conv_blocks):
    (x_ref, c2r_ref, s2r_ref, cm_ref, s1m_ref, s2m_ref, gmix_ref, win_ref, wqk_ref, wkva_ref, wg_ref, gqa_ref,
     wuq_ref, gqn_ref, gkva_ref, wuk_ref, wuv_ref, gkn_ref) = refs[:18]
    refs = refs[18:]
    if conv_blocks is not None:
        wdw_ref, bdw_ref, gcln_ref, bcln_ref = refs[:4]
        refs = refs[4:]
    u_ref, q_ref, k_ref, v_ref, srg_ref, qm_ref, km_ref, vm_ref, ckv_ref, kpe_ref, gates_ref = refs[:11]
    refs = refs[11:]
    o_glu, o_rq, o_rk, o_rv, o_rg, o_qa = offs
    if conv_blocks is not None:
        ca_ref, carry, ubuf, sbuf = refs

        @pl.when(pl.program_id(0) == 0)
        def _():
            carry[...] = jnp.zeros_like(carry)

    h = _rms(x_ref[...], gmix_ref[...]).astype(bf16)

    def proj(c0, c1):
        return jnp.dot(h, win_ref[:, c0:c1], preferred_element_type=f32)

    def proj_all(w_ref):
        return jnp.dot(h, w_ref[...], preferred_element_type=f32)

    glu = proj(o_glu, o_rq)
    half = (o_rq - o_glu) // 2
    u = glu[:, :half] * jax.nn.sigmoid(glu[:, half:])
    u_ref[...] = u

    if conv_blocks is not None:
        i = pl.program_id(0)
        tm = u.shape[0]
        ubuf[0:HALO_ROWS, :] = jnp.where(i % conv_blocks > 0, carry[...], 0.0)
        ubuf[HALO_ROWS:, :] = u
        carry[...] = u[tm - HALO_ROWS:, :]
        for step in _conv_ln_swish_steps(ubuf, sbuf, wdw_ref, bdw_ref, gcln_ref, bcln_ref, ca_ref, tm,
                                         wdw_ref.shape[0], min(CONV_CHUNK_ROWS, tm)):
            step()

    c2, s2 = c2r_ref[...], s2r_ref[...]
    rq = proj(o_rq, o_rk)
    rk = proj(o_rk, o_rv)
    for hh in range(RET_HEADS):
        sl = slice(hh * RET_DK, (hh + 1) * RET_DK)
        xq, xk = rq[:, sl], rk[:, sl]
        q_ref[:, sl] = (xq * c2 + pltpu.roll(xq, RET_DK // 2, 1) * s2).astype(bf16)
        k_ref[:, sl] = ((xk * c2 + pltpu.roll(xk, RET_DK // 2, 1) * s2) * (RET_DK ** -0.5)).astype(bf16)
    v_ref[...] = proj(o_rv, o_rg).astype(bf16)
    srg_ref[...] = jax.nn.silu(proj(o_rg, o_qa)).astype(bf16)

    cm, s1, s2m = cm_ref[...], s1m_ref[...], s2m_ref[...]
    qa_kr = proj_all(wqk_ref)
    q_lora = gqa_ref.shape[-1]
    c_q = _rms(qa_kr[:, :q_lora], gqa_ref[...]).astype(bf16)
    qm = jnp.dot(c_q, wuq_ref[...], preferred_element_type=f32)
    gqn = gqn_ref[...]
    for hh in range(MLA_HEADS):
        sl = slice(hh * LANE, (hh + 1) * LANE)
        r = _mla_rope(qm[:, sl], cm, s1, s2m)
        ss = jnp.sum(r * r, axis=-1, keepdims=True)
        qm_ref[:, sl] = (r * lax.rsqrt(ss / MLA_QK + EPS) * gqn).astype(bf16)

    ckv = _rms(proj_all(wkva_ref), gkva_ref[...])
    ckv_ref[...] = ckv
    cb = ckv.astype(bf16)
    kpe_slab = _mla_rope(qa_kr[:, q_lora:], cm, s1, s2m)
    kpe_ref[...] = kpe_slab[:, MLA_NOPE:MLA_QK]
    kn = jnp.dot(cb, wuk_ref[...], preferred_element_type=f32)
    gkn = gkn_ref[...]
    for hh in range(MLA_HEADS):
        sl = slice(hh * LANE, (hh + 1) * LANE)
        kk = kn[:, sl] + kpe_slab
        ss = jnp.sum(kk * kk, axis=-1, keepdims=True)
        km_ref[:, sl] = (kk * lax.rsqrt(ss / MLA_QK + EPS) * gkn).astype(bf16)
    vm_ref[...] = jnp.dot(cb, wuv_ref[...], preferred_element_type=f32).astype(bf16)

    gates_ref[...] = jax.nn.sigmoid(proj_all(wg_ref)).astype(bf16)

def _in_proj(x2d, tabs, l, W, offs, tm, n_pos_blocks, fuse_conv):
    M, D = x2d.shape
    conv_dim = (offs[1] - offs[0]) // 2
    n_gates = W['w_in_g'].shape[-1]
    kv_lora = W['w_in_kva'].shape[-1]
    rows = lambda n: pl.BlockSpec((tm, n), lambda i: (i, 0))
    tab = pl.BlockSpec((tm, LANE), lambda i: (i % n_pos_blocks, 0))
    wl = lambda a: _const_spec((None,) + a.shape[1:], lambda i: (l,) + (0,) * (a.ndim - 1))
    out_dims = [(conv_dim, f32), (RET_HEADS * RET_DK, bf16), (RET_HEADS * RET_DK, bf16), (RET_HEADS * RET_DV, bf16),
                (RET_HEADS * RET_DV, bf16), (MLA_HEADS * LANE, bf16), (MLA_HEADS * LANE, bf16),
                (MLA_HEADS * MLA_V, bf16), (kv_lora, f32), (MLA_ROPE, f32), (n_gates, bf16)]
    weights = [W['g_mix'], W['w_in_a'], W['w_in_qk'], W['w_in_kva'], W['w_in_g'], W['g_qa'], W['w_uq_p'],
               W['g_qn_p'], W['g_kva'], W['w_uk_p'], W['w_uv_b'],
               W['g_kn_p']]
    scratch = []
    if fuse_conv:
        assert tm >= HALO_ROWS
        weights += [W['w_dw'], W['b_dw'], W['g_cln'], W['b_cln']]
        out_dims.append((conv_dim, bf16))
        scratch = [pltpu.VMEM((HALO_ROWS, conv_dim), f32), pltpu.VMEM((HALO_ROWS + tm, conv_dim), f32),
                   pltpu.VMEM((SUBLANES, HALO_ROWS + tm, conv_dim), f32)]
    return pl.pallas_call(
        functools.partial(_in_proj_kernel, offs=offs, conv_blocks=n_pos_blocks if fuse_conv else None),
        grid=(M // tm,),
        in_specs=[rows(D)] + [tab] * 5 + [wl(a) for a in weights],
        out_specs=[rows(n) for n, _ in out_dims],
        out_shape=[jax.ShapeDtypeStruct((M, n), dt) for n, dt in out_dims],
        scratch_shapes=scratch,
        compiler_params=_params(("arbitrary",) if fuse_conv else ("parallel",)),
        name="in_proj",
    )(x2d, *tabs, *weights)


def _ln_swish(y, g, b):
    mu = jnp.mean(y, axis=-1, keepdims=True)
    d = y - mu
    var = jnp.mean(d * d, axis=-1, keepdims=True)
    z = d * lax.rsqrt(var + EPS) * g + b
    return z * jax.nn.sigmoid(z)


def _conv_ln_swish_steps(ubuf, sbuf, w_ref, b_ref, g_ref, bl_ref, o_ref, tc, width, rc):
    first = HALO_ROWS - (width - 1)
    C = ubuf.shape[-1]
    R = HALO_ROWS + tc

    def shifted_copies():
        sbuf[0] = ubuf[...]
        for ph in range(1, SUBLANES):
            sbuf[ph, 0:R - SUBLANES, :] = ubuf[ph:R - SUBLANES + ph, :]

    def chunk(r0):
        cols = []
        for c0 in range(0, C, LANE):
            acc = jnp.zeros((rc, LANE), f32) + b_ref[:, c0:c0 + LANE]
            for j in range(width):
                ph = (first + j) % SUBLANES
                base = first + j - ph + r0
                acc = acc + w_ref[j:j + 1, c0:c0 + LANE] * sbuf[ph, base:base + rc, c0:c0 + LANE]
            cols.append(acc)
        y = jnp.concatenate(cols, axis=-1)
        o_ref[r0:r0 + rc, :] = _ln_swish(y, g_ref[...], bl_ref[...]).astype(bf16)

    return [shifted_copies] + [functools.partial(chunk, r0) for r0 in range(0, tc, rc)]


def _conv_sample_kernel(st_ref, u_ref, w_ref, b_ref, g_ref, bl_ref, o_ref, *, width):
    y = u_ref[...] * w_ref[width - 1:width, :] + b_ref[...]
    for j in range(width - 1):
        y = y + st_ref[:, j, :] * w_ref[j:j + 1, :]
    o_ref[...] = _ln_swish(y, g_ref[...], bl_ref[...]).astype(bf16)


def _conv_sample(state, u2, l, W):
    Bs, C = u2.shape
    width = W['w_dw'].shape[1]
    wl = lambda a: pl.BlockSpec((None,) + a.shape[1:], lambda i: (l,) + (0,) * (a.ndim - 1))
    weights = [W['w_dw'], W['b_dw'], W['g_cln'], W['b_cln']]
    return pl.pallas_call(
        functools.partial(_conv_sample_kernel, width=width),
        grid=(1,),
        in_specs=[pl.BlockSpec((None,) + state.shape[1:], lambda i: (l, 0, 0, 0)),
                  pl.BlockSpec(u2.shape, lambda i: (0, 0))] + [wl(a) for a in weights],
        out_specs=pl.BlockSpec((Bs, C), lambda i: (0, 0)),
        out_shape=jax.ShapeDtypeStruct((Bs, C), bf16),
        compiler_params=_params(("arbitrary",)),
        name="conv_sample",
    )(state, u2, *weights)


def _group_norm_gate(o, g, gate):
    mu = jnp.mean(o, axis=-1, keepdims=True)
    d = o - mu
    var = jnp.mean(d * d, axis=-1, keepdims=True)
    return (d * lax.rsqrt(var + EPS) * g * gate).astype(bf16)


def _ret_prompt_kernel(gc_ref, q_ref, k_ref, v_ref, srg_ref, decay_ref, cross_ref, kdec_ref, g_ref,
                       o_ref, s_out_ref, s_ref):
    c = pl.program_id(1)

    @pl.when(c == 0)
    def _():
        s_ref[...] = jnp.zeros_like(s_ref)

    C = RET_CHUNK
    for hh in range(RET_HEADS):
        ks = slice(hh * RET_DK, (hh + 1) * RET_DK)
        vs = slice(hh * RET_DV, (hh + 1) * RET_DV)
        cr = jnp.concatenate([cross_ref[hh]] * (RET_DV // LANE), axis=-1)
        s_cur = s_ref[hh]
        for r0 in range(0, q_ref.shape[1], C):
            rows = slice(r0, r0 + C)
            q, k, v = q_ref[0, rows, ks], k_ref[0, rows, ks], v_ref[0, rows, vs]
            scores = lax.dot_general(q, k, _NT, preferred_element_type=f32) * decay_ref[hh]
            o = jnp.dot(scores.astype(bf16), v, preferred_element_type=f32)
            o = o + jnp.dot(q, s_cur.astype(bf16), preferred_element_type=f32) * cr
            kd = (k.astype(f32) * kdec_ref[hh]).astype(bf16)
            s_cur = gc_ref[hh] * s_cur + lax.dot_general(kd, v, _TN, preferred_element_type=f32)
            o_ref[0, rows, vs] = _group_norm_gate(o, g_ref[:, vs], srg_ref[0, rows, vs].astype(f32))
        s_ref[hh] = s_cur

    @pl.when(c == pl.num_programs(1) - 1)
    def _():
        s_out_ref[0] = s_ref[...]


def _ret_prompt(q3, k3, v3, srg3, l, W, tabs, tr):
    B, T, _ = q3.shape
    gc, decay, cross, kdec = tabs
    blk = lambda n: pl.BlockSpec((1, tr, n), lambda b, c: (b, c, 0))
    tab = _const_spec((RET_HEADS, RET_CHUNK, LANE), lambda b, c: (0, 0, 0))
    return pl.pallas_call(
        _ret_prompt_kernel,
        grid=(B, T // tr),
        in_specs=[pl.BlockSpec(memory_space=pltpu.SMEM), blk(RET_HEADS * RET_DK), blk(RET_HEADS * RET_DK),
                  blk(RET_HEADS * RET_DV), blk(RET_HEADS * RET_DV), tab, tab, tab,
                  _const_spec((None, 1, RET_HEADS * RET_DV), lambda b, c: (l, 0, 0))],
        out_specs=[blk(RET_HEADS * RET_DV),
                   pl.BlockSpec((1, RET_HEADS, RET_DK, RET_DV), lambda b, c: (b, 0, 0, 0))],
        out_shape=[jax.ShapeDtypeStruct((B, T, RET_HEADS * RET_DV), bf16),
                   jax.ShapeDtypeStruct((B, RET_HEADS, RET_DK, RET_DV), f32)],
        scratch_shapes=[pltpu.VMEM((RET_HEADS, RET_DK, RET_DV), f32)],
        compiler_params=_params(("parallel", "arbitrary")),
        name="ret_prompt",
    )(gc, q3, k3, v3, srg3, decay, cross, kdec, W['g_rgn'])


def _row_to_col(row, n):
    eye = lax.broadcasted_iota(jnp.int32, (n, n), 0) == lax.broadcasted_iota(jnp.int32, (n, n), 1)
    return jnp.sum(jnp.where(eye, jnp.broadcast_to(row, (n, n)), 0.0), axis=-1, keepdims=True)


def _ret_sample_kernel(g1_ref, q_ref, k_ref, v_ref, srg_ref, s_ref, g_ref, o_ref, s_out_ref):
    for i in range(q_ref.shape[0]):
        for hh in range(RET_HEADS):
            ks = slice(hh * RET_DK, (hh + 1) * RET_DK)
            vs = slice(hh * RET_DV, (hh + 1) * RET_DV)
            qcol = _row_to_col(q_ref[i, :, ks].astype(f32), RET_DK)
            kcol = _row_to_col(k_ref[i, :, ks].astype(f32), RET_DK)
            v = v_ref[i, :, vs].astype(f32)
            s_new = g1_ref[hh] * s_ref[i, hh] + kcol * v
            s_out_ref[i, hh] = s_new
            o = jnp.sum(qcol * s_new, axis=0, keepdims=True)
            o_ref[i, :, vs] = _group_norm_gate(o, g_ref[:, vs], srg_ref[i, :, vs].astype(f32))


def _ret_sample(q2, k2, v2, srg2, state, l, W, g1):
    Bs = q2.shape[0]
    nb = next(c for c in (4, 2, 1) if Bs % c == 0)
    r3 = lambda a: a.reshape(Bs, 1, a.shape[-1])
    row = lambda n: pl.BlockSpec((nb, 1, n), lambda b: (b, 0, 0))
    sblk = (nb, RET_HEADS, RET_DK, RET_DV)
    o, s_new = pl.pallas_call(
        _ret_sample_kernel,
        grid=(Bs // nb,),
        in_specs=[pl.BlockSpec(memory_space=pltpu.SMEM), row(RET_HEADS * RET_DK), row(RET_HEADS * RET_DK),
                  row(RET_HEADS * RET_DV), row(RET_HEADS * RET_DV),
                  pl.BlockSpec((None,) + sblk, lambda b: (l, b, 0, 0, 0)),
                  pl.BlockSpec((None, 1, RET_HEADS * RET_DV), lambda b: (l, 0, 0))],
        out_specs=[row(RET_HEADS * RET_DV), pl.BlockSpec(sblk, lambda b: (b, 0, 0, 0))],
        out_shape=[jax.ShapeDtypeStruct((Bs, 1, RET_HEADS * RET_DV), bf16),
                   jax.ShapeDtypeStruct((Bs, RET_HEADS, RET_DK, RET_DV), f32)],
        compiler_params=_params(("parallel",)),
        name="ret_sample",
    )(g1, r3(q2), r3(k2), r3(v2), r3(srg2), state, W['g_rgn'])
    return o.reshape(Bs, -1), s_new


def _att_prompt_kernel(qi_ref, ki_ref, q_ref, k_ref, v_ref, o_ref, m_ref, l_ref, acc_ref):
    s_idx = pl.program_id(1)
    qi, ki = qi_ref[s_idx], ki_ref[s_idx]
    tq, tk = q_ref.shape[1], k_ref.shape[1]
    n_lt = tk // LANE

    @pl.when(ki == 0)
    def _():
        m_ref[...] = jnp.full_like(m_ref, MASK_VALUE)
        l_ref[...] = jnp.zeros_like(l_ref)
        acc_ref[...] = jnp.zeros_like(acc_ref)

    def step(diagonal):
        for hd in range(MLA_HEADS):
            v = v_ref[0, :, (hd // 2) * LANE:(hd // 2 + 1) * LANE]
            sl = slice(hd * LANE, (hd + 1) * LANE)
            s = lax.dot_general(q_ref[0, :, sl], k_ref[0, :, sl], _NT, preferred_element_type=f32)
            if diagonal:
                row = lax.broadcasted_iota(jnp.int32, (tq, tk), 0)
                col = lax.broadcasted_iota(jnp.int32, (tq, tk), 1)
                s = jnp.where(col <= row, s, MASK_VALUE)
            m_prev = m_ref[hd]
            smax = s[:, 0:LANE]
            for c in range(1, n_lt):
                smax = jnp.maximum(smax, s[:, c * LANE:(c + 1) * LANE])
            m_new = jnp.maximum(m_prev, jnp.max(smax, axis=-1, keepdims=True))
            a = jnp.exp2(m_prev - m_new)
            p = jnp.exp2(s - jnp.concatenate([m_new] * n_lt, axis=-1))
            psum = p[:, 0:LANE]
            for c in range(1, n_lt):
                psum = psum + p[:, c * LANE:(c + 1) * LANE]
            l_ref[hd] = a * l_ref[hd] + psum
            acc_ref[hd] = a * acc_ref[hd] + jnp.dot(p.astype(bf16), v, preferred_element_type=f32)
            m_ref[hd] = m_new

    @pl.when(ki < qi)
    def _():
        step(False)

    @pl.when(ki == qi)
    def _():
        step(True)
        lane = lax.broadcasted_iota(jnp.int32, (tq, LANE), 1)
        for g in range(MLA_HEADS // 2):
            l0 = jnp.sum(l_ref[2 * g], axis=-1, keepdims=True)
            l1 = jnp.sum(l_ref[2 * g + 1], axis=-1, keepdims=True)
            o = jnp.where(lane < MLA_V, acc_ref[2 * g] / l0, acc_ref[2 * g + 1] / l1)
            o_ref[0, :, g * LANE:(g + 1) * LANE] = o.astype(bf16)


def _att_prompt(qm3, km3, vm3, tq):
    B, T, _ = qm3.shape
    nq = T // tq
    H = MLA_HEADS
    pairs = [(qi, ki) for qi in range(nq) for ki in range(qi + 1)]
    qi_tab = jnp.asarray([p[0] for p in pairs], jnp.int32)
    ki_tab = jnp.asarray([p[1] for p in pairs], jnp.int32)
    grid_spec = pltpu.PrefetchScalarGridSpec(
        num_scalar_prefetch=2,
        grid=(B, len(pairs)),
        in_specs=[pl.BlockSpec((1, tq, H * LANE), lambda b, s, qi, ki: (b, qi[s], 0)),
                  pl.BlockSpec((1, tq, H * LANE), lambda b, s, qi, ki: (b, ki[s], 0)),
                  pl.BlockSpec((1, tq, H * MLA_V), lambda b, s, qi, ki: (b, ki[s], 0))],
        out_specs=pl.BlockSpec((1, tq, H * MLA_V), lambda b, s, qi, ki: (b, qi[s], 0)),
        scratch_shapes=[pltpu.VMEM((H, tq, LANE), f32), pltpu.VMEM((H, tq, LANE), f32),
                        pltpu.VMEM((H, tq, LANE), f32)],
    )
    return pl.pallas_call(
        _att_prompt_kernel,
        grid_spec=grid_spec,
        out_shape=jax.ShapeDtypeStruct((B, T, H * MLA_V), bf16),
        compiler_params=_params(("parallel", "arbitrary")),
        name="att_prompt",
    )(qi_tab, ki_tab, qm3, km3, vm3)


def _hi_lo(x):
    hi = x.astype(bf16).astype(f32)
    return jnp.concatenate([hi, x - hi], axis=0).astype(bf16)


def _rows_select(rows, n=None):
    n, w = n or len(rows), rows[0].shape[-1]
    sub = lax.broadcasted_iota(jnp.int32, (n, w), 0)
    out = jnp.zeros((n, w), rows[0].dtype)
    for i, r in enumerate(rows):
        out = jnp.where(sub == i, jnp.broadcast_to(r, (n, w)), out)
    return out


def _att_sample_kernel(pt_ref, ckv_hbm, kpe_hbm, qm_ref, kmn_ref, ckvn_ref, gkn_ref, wukt_ref, wuktp_ref, wuv_ref,
                       o_ref, ckv_buf, kpe_buf, sem_c, sem_k, aq_ref, cbb_ref, *, l, pp, nj, nr):
    b = pl.program_id(0)
    nb = pl.num_programs(0)
    H = MLA_HEADS
    page = ckv_hbm.shape[2]

    def copies(bb, r, j, slot):
        out = []
        for i in range(pp):
            pg = pt_ref[bb * nr + r, j * pp + i]
            out.append(pltpu.make_async_copy(ckv_hbm.at[l, pg], ckv_buf.at[r, slot, pl.ds(i * page, page)],
                                             sem_c.at[r, slot]))
            out.append(pltpu.make_async_copy(kpe_hbm.at[l, pg], kpe_buf.at[r, slot, :, pl.ds(i * page, page)],
                                             sem_k.at[r, slot]))
        return out

    @pl.when(b == 0)
    def _():
        for r in range(nr):
            for c in copies(0, r, 0, 0) + copies(0, r, 1, 1):
                c.start()

    gkn = jnp.concatenate([gkn_ref[...]] * H, axis=-1)
    sub = lax.broadcasted_iota(jnp.int32, (2 * H, H * LANE), 0)
    head = lax.broadcasted_iota(jnp.int32, (2 * H, H * LANE), 1) // LANE
    qr = []
    for r in range(nr):
        qg = qm_ref[r].astype(f32) * gkn
        qbd = jnp.where(sub == head, jnp.broadcast_to(qg, (2 * H, H * LANE)), 0.0).astype(bf16)
        qabs = jnp.dot(qbd, wuktp_ref[...], preferred_element_type=f32)[0:H]
        aq_ref[r, 0:H * MLA_NOPE, :] = wukt_ref[...]
        aq_ref[r, H * MLA_NOPE:, :] = _hi_lo(qabs)
        qr.append(_rows_select([qg[:, h * LANE + MLA_NOPE:h * LANE + MLA_QK] for h in range(H)],
                               2 * H).astype(bf16))

    def scores(r, j):
        slot = j % 2
        cbb_ref[r, slot] = ckv_buf[r, slot].astype(bf16)
        cb = cbb_ref[r, slot]
        kt = kpe_buf[r, slot]
        x = lax.dot_general(aq_ref[r], cb, _NT, preferred_element_type=f32)
        ss = jnp.concatenate(
            [jnp.sum(jnp.square(x[h * MLA_NOPE:(h + 1) * MLA_NOPE]), axis=0, keepdims=True) for h in range(H)],
            axis=0)
        ss_rope = jnp.sum(kt * kt, axis=0, keepdims=True)
        s_rope = jnp.dot(qr[r], kt.astype(bf16), preferred_element_type=f32)[0:H]
        s_nope = x[H * MLA_NOPE:H * MLA_NOPE + H] + x[H * MLA_NOPE + H:]
        s = (s_nope + s_rope) * lax.rsqrt((ss + ss_rope) / MLA_QK + EPS)
        if j + 2 < nj:
            for c in copies(b, r, j + 2, slot):
                c.start()
        else:
            for c in copies(jnp.minimum(b + 1, nb - 1), r, j + 2 - nj, slot):
                c.start()
        return s

    def step_scores(j):
        for r in range(nr):
            for c in copies(b, r, j, j % 2):
                c.wait()
        return [scores(r, j) for r in range(nr)]

    def accumulate(r, j, s, m, lsum, ctx):
        m_new = jnp.maximum(m, jnp.max(s, axis=-1, keepdims=True))
        a = jnp.exp2(m - m_new)
        p = jnp.exp2(s - m_new)
        lsum = a * lsum + jnp.sum(p, axis=-1, keepdims=True)
        p16 = jnp.concatenate([p, jnp.zeros_like(p)], axis=0).astype(bf16)
        ctx = a * ctx + jnp.dot(p16, cbb_ref[r, j % 2], preferred_element_type=f32)[0:H]
        return m_new, lsum, ctx

    state = [(jnp.full((H, 1), MASK_VALUE, f32), jnp.zeros((H, 1), f32), jnp.zeros((H, ckvn_ref.shape[-1]), f32))
             for _ in range(nr)]
    s_prev = step_scores(0)
    for j in range(1, nj):
        s_cur = step_scores(j)
        state = [accumulate(r, j - 1, s_prev[r], *state[r]) for r in range(nr)]
        s_prev = s_cur
    state = [accumulate(r, nj - 1, s_prev[r], *state[r]) for r in range(nr)]

    for r in range(nr):
        m, lsum, ctx = state[r]
        prod = qm_ref[r].astype(f32) * kmn_ref[r].astype(f32)
        s_new = jnp.sum(_rows_select([prod[:, h * LANE:(h + 1) * LANE] for h in range(H)]), axis=-1,
                        keepdims=True)
        m_fin = jnp.maximum(m, s_new)
        a = jnp.exp2(m - m_fin)
        pn = jnp.exp2(s_new - m_fin)
        l_fin = a * lsum + pn
        c_new = ckvn_ref[r].astype(bf16).astype(f32)
        ctx = (a * ctx + pn.astype(bf16).astype(f32) * c_new) / l_fin
        both = jnp.dot(_hi_lo(ctx), wuv_ref[...], preferred_element_type=f32)
        full = both[0:H] + both[H:]
        sub_o = lax.broadcasted_iota(jnp.int32, full.shape, 0)
        head_o = lax.broadcasted_iota(jnp.int32, full.shape, 1) // MLA_V
        o_ref[r] = jnp.sum(jnp.where(sub_o == head_o, full, 0.0), axis=0, keepdims=True).astype(bf16)

    @pl.when(b == nb - 1)
    def _():
        for r in range(nr):
            for c in copies(b, r, 0, 0) + copies(b, r, 1, 1):
                c.wait()


def _att_sample(qm2, kmn2, ckvn2, cache_ckv, cache_kpe_t, page_table, l, W, pp, nr):
    Bs = qm2.shape[0]
    n_pages = page_table.shape[1]
    page, n_lat = cache_ckv.shape[2], cache_ckv.shape[3]
    rope = cache_kpe_t.shape[2]
    H = MLA_HEADS
    nj = n_pages // pp
    assert nj % 2 == 0, "the two staging slots alternate by step parity across grid steps"
    assert Bs % nr == 0
    r3 = lambda a: a.reshape(Bs, 1, a.shape[-1])
    row = lambda n: pl.BlockSpec((nr, 1, n), lambda b, pt: (b, 0, 0))
    wl = lambda a: _const_spec((None,) + a.shape[1:], lambda b, pt: (l,) + (0,) * (a.ndim - 1))
    weights = [W['g_kn_p'], W['w_ukt_b'], W['w_ukt_p'], W['w_uv_b']]
    grid_spec = pltpu.PrefetchScalarGridSpec(
        num_scalar_prefetch=1,
        grid=(Bs // nr,),
        in_specs=([pl.BlockSpec(memory_space=pl.ANY), pl.BlockSpec(memory_space=pl.ANY)]
                  + [row(H * LANE), row(H * LANE), row(n_lat)] + [wl(a) for a in weights]),
        out_specs=row(H * MLA_V),
        scratch_shapes=[pltpu.VMEM((nr, 2, pp * page, n_lat), f32), pltpu.VMEM((nr, 2, rope, pp * page), f32),
                        pltpu.SemaphoreType.DMA((nr, 2)), pltpu.SemaphoreType.DMA((nr, 2)),
                        pltpu.VMEM((nr, H * MLA_NOPE + 2 * H, n_lat), bf16),
                        pltpu.VMEM((nr, 2, pp * page, n_lat), bf16)],
    )
    o = pl.pallas_call(
        functools.partial(_att_sample_kernel, l=l, pp=pp, nj=nj, nr=nr),
        grid_spec=grid_spec,
        out_shape=jax.ShapeDtypeStruct((Bs, 1, H * MLA_V), bf16),
        compiler_params=_params(("arbitrary",)),
        name="att_sample",
    )(page_table, cache_ckv, cache_kpe_t, r3(qm2), r3(kmn2), r3(ckvn2), *weights)
    return o.reshape(Bs, -1)


def _out_kernel(x_ref, ca_ref, ra_ref, om_ref, gates_ref, p_ref, wco_ref, wro_ref, wmo_ref, wout_ref, gffn_ref,
                wg_ref, wu_ref, wd_ref, gple_ref, wpg_ref, wpp_ref, o_ref):
    D = x_ref.shape[-1]
    dot = lambda a, b: jnp.dot(a, b, preferred_element_type=f32)
    gate = lambda i: gates_ref[:, i * D:(i + 1) * D].astype(f32)
    mix = (gate(0) * dot(ca_ref[...], wco_ref[...]) + gate(1) * dot(ra_ref[...], wro_ref[...])
           + gate(2) * dot(om_ref[...], wmo_ref[...]))
    x = x_ref[...] + dot(mix.astype(bf16), wout_ref[...])
    hf = _rms(x, gffn_ref[...]).astype(bf16)
    ff = jax.nn.silu(dot(hf, wg_ref[...])) * dot(hf, wu_ref[...])
    x = x + dot(ff.astype(bf16), wd_ref[...])
    hp = _rms(x, gple_ref[...]).astype(bf16)
    pg = jax.nn.sigmoid(dot(hp, wpg_ref[...]))
    o_ref[...] = x + pg * dot(p_ref[...].astype(bf16), wpp_ref[...])


def _out_stage(x2d, ca, ra, om, gates, p3, l, W, tm):
    M, D = x2d.shape
    rows = lambda n: pl.BlockSpec((tm, n), lambda i: (i, 0))
    wl = lambda a: _const_spec((None,) + a.shape[1:], lambda i: (l,) + (0,) * (a.ndim - 1))
    weights = [W['w_conv_out_b'], W['w_ret_out_b'], W['w_mla_out_b'], W['w_out_b'], W['g_ffn'], W['w_gate_b'],
               W['w_up_b'], W['w_down_b'], W['g_ple'], W['w_ple_gate_b'], W['w_ple_proj_b']]
    return pl.pallas_call(
        _out_kernel,
        grid=(M // tm,),
        in_specs=[rows(D), rows(ca.shape[1]), rows(ra.shape[1]), rows(om.shape[1]), rows(gates.shape[1]),
                  pl.BlockSpec((None, tm, p3.shape[-1]), lambda i: (l, i, 0))] + [wl(a) for a in weights],
        out_specs=rows(D),
        out_shape=jax.ShapeDtypeStruct((M, D), f32),
        compiler_params=_params(("parallel",)),
        name="out_stage",
    )(x2d, ca, ra, om, gates, p3, *weights)


def _rope_tables(pos):
    n = pos.shape[0]
    posf = pos.astype(f32)[:, None]

    def cs(half):
        inv = ROPE_BASE ** (-jnp.arange(half, dtype=f32) / half)
        ang = posf * inv[None, :]
        return jnp.cos(ang), jnp.sin(ang)

    c, s = cs(RET_DK // 2)
    c2r = jnp.concatenate([c, c], axis=-1)
    s2r = jnp.concatenate([-s, s], axis=-1)
    c, s = cs(MLA_ROPE // 2)
    hr = MLA_ROPE // 2
    z = lambda w: jnp.zeros((n, w), f32)
    cm = jnp.concatenate([jnp.ones((n, MLA_NOPE), f32), c, c, z(LANE - MLA_QK)], axis=-1)
    s1m = jnp.concatenate([z(MLA_NOPE + hr), s, z(LANE - MLA_QK)], axis=-1)
    s2m = jnp.concatenate([z(MLA_NOPE), -s, z(LANE - MLA_NOPE - hr)], axis=-1)
    return c2r, s2r, cm, s1m, s2m


def _retention_tables(C):
    log_gamma = jnp.log(1.0 - 2.0 ** (-5.0 - jnp.arange(RET_HEADS, dtype=f32)))
    idx = jnp.arange(C, dtype=f32)
    diff = idx[:, None] - idx[None, :]
    decay = jnp.where(diff >= 0, jnp.exp(jnp.maximum(diff, 0.0)[None] * log_gamma[:, None, None]), 0.0)
    cross = jnp.exp((idx + 1.0)[:, None] * log_gamma[None, :])
    kdec = jnp.exp((C - 1.0 - idx)[:, None] * log_gamma[None, :])
    gc = jnp.exp(C * log_gamma)
    lanes = lambda a: jnp.broadcast_to(a.T[:, :, None], (RET_HEADS, C, LANE))
    return gc, decay, lanes(cross), lanes(kdec)


def _pad_heads(w, heads, width):
    w = w.reshape(w.shape[:-1] + (heads, width))
    w = jnp.pad(w, [(0, 0)] * (w.ndim - 1) + [(0, LANE - width)])
    return w.reshape(w.shape[:-2] + (heads * LANE,))


def kernel(x_prompt, x_sample, p_prompt, p_sample, cache_ckv, cache_kpe, state_ret, state_conv, page_table, g_mix, w_in, w_dw, b_dw, g_cln, b_cln, w_conv_out, g_rgn, w_ret_out, g_qa, w_uq, g_qn, g_kva, w_uk, w_uv, g_kn, w_mla_out, w_out, g_ffn, w_gate, w_up, w_down, g_ple, w_ple_gate, w_ple_proj):
    B, T, D = x_prompt.shape
    Bs, Ts, _ = x_sample.shape
    assert Ts == 1, "the sample group decodes one token per request"
    depth = w_in.shape[0]
    conv_dim, width = w_dw.shape[2], w_dw.shape[1]
    q_lora, kv_lora = g_qa.shape[1], g_kva.shape[1]
    page = cache_ckv.shape[2]
    past_len = page_table.shape[1] * page
    assert T % RET_CHUNK == 0 and width - 1 <= HALO_ROWS <= T

    sizes = (2 * conv_dim, RET_HEADS * RET_DK, RET_HEADS * RET_DK, RET_HEADS * RET_DV, RET_HEADS * RET_DV,
             q_lora, kv_lora, MLA_ROPE, 3 * D)
    src = np.concatenate([[0], np.cumsum(sizes)]).tolist()
    offs = tuple(src[:6])
    kr_slab = jnp.pad(w_in[..., src[7]:src[8]], [(0, 0), (0, 0), (MLA_NOPE, LANE - MLA_QK)])

    row = lambda g: g[:, None, :]
    lane_pad = lambda g: jnp.pad(g, [(0, 0), (0, LANE - g.shape[-1])])[:, None, :]
    W = dict(
        g_mix=row(g_mix), g_qa=row(g_qa), g_kva=row(g_kva),
        w_in_a=w_in[..., :src[5]].astype(bf16),
        w_in_qk=jnp.concatenate([w_in[..., src[5]:src[6]], kr_slab], axis=-1).astype(bf16),
        w_in_kva=w_in[..., src[6]:src[7]].astype(bf16),
        w_in_g=w_in[..., src[8]:].astype(bf16),
        w_uq_p=_pad_heads(w_uq, MLA_HEADS, MLA_QK).astype(bf16),
        g_qn_p=lane_pad(g_qn * (MLA_QK ** -0.5 * LOG2_E)),
        w_uk_p=_pad_heads(w_uk, MLA_HEADS, MLA_NOPE).astype(bf16),
        g_kn_p=lane_pad(g_kn),
        w_uv_b=w_uv.astype(bf16),
        w_ukt_b=jnp.swapaxes(w_uk, 1, 2).astype(bf16),
        w_ukt_p=jnp.swapaxes(_pad_heads(w_uk, MLA_HEADS, MLA_NOPE), 1, 2).astype(bf16),
        w_dw=w_dw, b_dw=row(b_dw), g_cln=row(g_cln), b_cln=row(b_cln), g_rgn=row(g_rgn),
        w_conv_out_b=w_conv_out.astype(bf16), w_ret_out_b=w_ret_out.astype(bf16),
        w_mla_out_b=w_mla_out.astype(bf16), w_out_b=w_out.astype(bf16), g_ffn=row(g_ffn),
        w_gate_b=w_gate.astype(bf16), w_up_b=w_up.astype(bf16), w_down_b=w_down.astype(bf16),
        g_ple=row(g_ple), w_ple_gate_b=w_ple_gate.astype(bf16), w_ple_proj_b=w_ple_proj.astype(bf16),
    )

    tabs_p = _rope_tables(jnp.arange(T, dtype=jnp.int32))
    tabs_s = _rope_tables(jnp.full((Bs,), past_len, jnp.int32))
    ret_tabs = _retention_tables(RET_CHUNK)
    g1 = _retention_tables(1)[0]

    tm = min(ROW_TILE, T)
    tq = min(ATT_TILE, T)
    tr = min(RET_TILE, T)
    n_pages = page_table.shape[1]
    pp = next(c for c in (PAGES_PER_STEP, 8, 4, 2, 1) if n_pages % (2 * c) == 0)
    nr = REQUESTS_PER_STEP if Bs % REQUESTS_PER_STEP == 0 else 1

    cache_kpe_t = jnp.swapaxes(cache_kpe, 2, 3)

    xp = x_prompt.reshape(B * T, D)
    xs = x_sample.reshape(Bs, D)
    pp3 = p_prompt.reshape(depth, B * T, -1)
    ps3 = p_sample.reshape(depth, Bs, -1)
    outs = [[] for _ in range(8)]
    for l in range(depth):
        u, q, k, v, srg, qm, km, vm, ckv, kpe, gates, ca = _in_proj(xp, tabs_p, l, W, offs, tm, T // tm, True)
        s3 = lambda a: a.reshape(B, T, a.shape[-1])
        u3 = s3(u)
        ra, s_new = _ret_prompt(s3(q), s3(k), s3(v), s3(srg), l, W, ret_tabs, tr)
        om = _att_prompt(s3(qm), s3(km), s3(vm), tq)
        xp = _out_stage(xp, ca, ra.reshape(B * T, -1), om.reshape(B * T, -1), gates, pp3, l, W, tm)
        outs[0].append(s3(ckv)); outs[1].append(s3(kpe)); outs[4].append(s_new)
        outs[6].append(u3[:, T - (width - 1):])

        u, q, k, v, srg, qm, km, vm, ckv, kpe, gates = _in_proj(xs, tabs_s, l, W, offs, Bs, 1, False)
        ca = _conv_sample(state_conv, u, l, W)
        ra, s_new = _ret_sample(q, k, v, srg, state_ret, l, W, g1)
        om = _att_sample(qm, km, ckv, cache_ckv, cache_kpe_t, page_table, l, W, pp, nr)
        xs = _out_stage(xs, ca, ra, om, gates, ps3, l, W, Bs)
        outs[2].append(ckv[:, None]); outs[3].append(kpe[:, None]); outs[5].append(s_new)
        outs[7].append(jnp.concatenate([state_conv[l][:, 1:], u[:, None]], axis=1))

    st = [jnp.stack(o) for o in outs]
    return (xp.reshape(B, T, D), xs.reshape(Bs, Ts, D), st[0], st[1], st[2], st[3], st[4], st[5], st[6], st[7])
```
